```python
import math
import jax
import jax.numpy as jnp
from jax import lax
import numpy as np

D_MODEL = 1024
BATCH = 32
SEQ = 2048
DEPTH = 2
DEC_BATCH = 8
DEC_SEQ = 32
PAST_LEN = 4096

CHUNK = 64
N_META = 16
H_RET = 4
DK_RET = D_MODEL // 8
DV_RET = 2 * DK_RET
H_SB = 8
D_SB = D_MODEL // 8
Q_BLOCK = 128
POOL_WINDOWS = (2, 4, 8, 16)
N_POOL_GROUPS = 4
D_POOL = D_MODEL
DG_POOL = D_POOL // N_POOL_GROUPS
POOL_BUF = max(POOL_WINDOWS) - 1
N_BRANCH = 3
D_BRANCH = D_MODEL
D_FF = ((8 * D_MODEL // 3 + 127) // 128) * 128
ROPE_BASE = 10000.0
LN_EPS = 1e-5
ALPHA = (2 * DEPTH) ** 0.25
BETA = (8 * DEPTH) ** -0.25
IN_SPLITS = (H_RET * DK_RET, H_RET * DK_RET, H_RET * DV_RET, H_RET * DV_RET,
             H_SB * D_SB, H_SB * D_SB, H_SB * D_SB, D_POOL, N_BRANCH * D_MODEL)
D_IN = sum(IN_SPLITS)

kernel_name = 'hybrid_retention_stickbreak_pool_stream'


def layer_norm(x, g, b):
    xf = x.astype(jnp.float32)
    mu = xf.mean(-1, keepdims=True)
    var = jnp.square(xf - mu).mean(-1, keepdims=True)
    return ((xf - mu) * lax.rsqrt(var + LN_EPS) * g + b).astype(x.dtype)


def head_norm(x, g):
    xf = x.astype(jnp.float32)
    mu = xf.mean(-1, keepdims=True)
    var = jnp.square(xf - mu).mean(-1, keepdims=True)
    return ((xf - mu) * lax.rsqrt(var + LN_EPS) * g).astype(x.dtype)


def swiglu(x, w_up, w_down):
    gate, up = jnp.split(x @ w_up, 2, axis=-1)
    return (jax.nn.silu(gate) * up) @ w_down


def rotary(x, pos0):
    T = x.shape[1]
    half = x.shape[-1] // 2
    inv_freq = ROPE_BASE ** (-jnp.arange(half, dtype=jnp.float32) / half)
    ang = (pos0 + jnp.arange(T, dtype=jnp.float32))[:, None] * inv_freq[None, :]
    cos = jnp.cos(ang)[None, :, None, :]
    sin = jnp.sin(ang)[None, :, None, :]
    xf = x.astype(jnp.float32)
    x1, x2 = xf[..., :half], xf[..., half:]
    return jnp.concatenate([x1 * cos - x2 * sin, x2 * cos + x1 * sin], axis=-1)


def retention_log_decay():
    return jnp.log1p(-jnp.exp(jnp.linspace(math.log(1.0 / 32), math.log(1.0 / 512), H_RET,
                                           dtype=jnp.float32)))


def retention(q, k, v, s0):
    B, T, H, _ = q.shape
    L = min(CHUNK, T)
    n = T // L
    lg = retention_log_decay()
    i = jnp.arange(L, dtype=jnp.float32)
    diff = i[:, None] - i[None, :]
    d_intra = jnp.where(diff >= 0, jnp.exp(lg[:, None, None] * jnp.maximum(diff, 0.0)), 0.0)
    d_q = jnp.exp(lg[None, :] * (i[:, None] + 1.0))
    d_k = jnp.exp(lg[None, :] * (L - 1.0 - i[:, None]))
    d_c = jnp.exp(lg * L)

    def chunks(a):
        return a.astype(jnp.float32).reshape(B, n, L, H, a.shape[-1]).transpose(1, 0, 2, 3, 4)

    def step(s, xs):
        qc, kc, vc = xs
        att = jnp.einsum('blhk,bmhk->bhlm', qc, kc) * d_intra[None]
        o = (jnp.einsum('bhlm,bmhv->blhv', att, vc)
             + jnp.einsum('blhk,bhkv->blhv', qc, s) * d_q[None, :, :, None])
        s = s * d_c[None, :, None, None] + jnp.einsum('bmhk,bmhv->bhkv', kc * d_k[None, :, :, None], vc)
        return s, o

    s, o = lax.scan(step, s0.astype(jnp.float32), (chunks(q), chunks(k), chunks(v)))
    o = o.transpose(1, 0, 2, 3, 4).reshape(B, T, H, v.shape[-1])
    return o.astype(v.dtype), s.astype(s0.dtype)


def stick_breaking(q, k_all, v_all, q_start):
    B, T, H, D = q.shape
    S = k_all.shape[1]
    qb = min(Q_BLOCK, T)
    nb = T // qb
    scale = D ** -0.5
    key_idx = jnp.arange(S)
    q_blocks = q.reshape(B, nb, qb, H, D).transpose(1, 0, 2, 3, 4)

    def block(args):
        qblk, b = args
        z = jnp.einsum('bqhd,bshd->bhqs', qblk, k_all, preferred_element_type=jnp.float32) * scale
        q_idx = q_start + b * qb + jnp.arange(qb)
        visible = (key_idx[None, :] < q_idx[:, None])[None, None]
        log_stay = jnp.where(visible, jax.nn.log_sigmoid(-z), 0.0)
        later = lax.cumsum(log_stay, axis=3, reverse=True) - log_stay
        w = jnp.where(visible, jnp.exp(jax.nn.log_sigmoid(z) + later), 0.0)
        return jnp.einsum('bhqs,bshd->bqhd', w.astype(v_all.dtype), v_all,
                          preferred_element_type=jnp.float32).astype(v_all.dtype)

    out = lax.map(block, (q_blocks, jnp.arange(nb)))
    return out.transpose(1, 0, 2, 3, 4).reshape(B, T, H, D)


def pool_mixer(u, buf, mix_w, scale):
    B, T, C = u.shape
    Lb = buf.shape[1]
    z = jnp.concatenate([buf, u], axis=1)
    csum = jnp.concatenate([jnp.zeros((B, 1, C), jnp.float32),
                            jnp.cumsum(z.astype(jnp.float32), axis=1)], axis=1)
    idx = Lb + jnp.arange(T)
    hi = csum[:, Lb + 1:]
    uf = u.astype(jnp.float32)
    parts = []
    for g, w in enumerate(POOL_WINDOWS):
        sl = slice(g * DG_POOL, (g + 1) * DG_POOL)
        lo = jnp.maximum(idx + 1 - w, 0)
        cnt = jnp.minimum(idx + 1, w).astype(jnp.float32)
        parts.append((hi[..., sl] - csum[:, lo, sl]) / cnt[None, :, None] - uf[..., sl])
    pooled = jnp.stack(parts, axis=2).astype(u.dtype)
    mixed = jnp.einsum('btgc,gcd->btgd', pooled, mix_w).reshape(B, T, C) * scale
    return mixed, z[:, -POOL_BUF:]


def trunk_layer(h, past_k, past_v, s0, pool_buf, pos0, lp):
    (w_in, ret_g, pool_w, pool_scale, w_branch, w_out,
     up1, down1, up2, down2, ln_g, ln_b) = lp
    B, T, _ = h.shape
    h = layer_norm(ALPHA * h + 0.5 * swiglu(h, up1, down1), ln_g[0], ln_b[0])
    points = np.cumsum(IN_SPLITS)[:-1].tolist()
    q_r, k_r, v_r, g_r, q_s, k_s, v_s, u, gates = jnp.split(h @ w_in, points, axis=-1)
    q_r = rotary(q_r.reshape(B, T, H_RET, DK_RET), pos0)
    k_r = rotary(k_r.reshape(B, T, H_RET, DK_RET), pos0) * (DK_RET ** -0.5)
    o_r, s_new = retention(q_r, k_r, v_r.reshape(B, T, H_RET, DV_RET), s0)
    o_r = head_norm(o_r, ret_g).reshape(B, T, H_RET * DV_RET) * jax.nn.silu(g_r)
    k_s = k_s.reshape(B, T, H_SB, D_SB)
    v_s = v_s.reshape(B, T, H_SB, D_SB)
    k_all = jnp.concatenate([past_k, k_s], axis=1)
    v_all = jnp.concatenate([past_v, v_s], axis=1)
    o_s = stick_breaking(q_s.reshape(B, T, H_SB, D_SB), k_all, v_all, past_k.shape[1])
    o_s = o_s.reshape(B, T, H_SB * D_SB)
    o_p, buf_new = pool_mixer(u, pool_buf, pool_w, pool_scale)
    branch = jnp.stack([o_r, o_s, o_p], axis=2)
    proj_b = jnp.einsum('btnc,ncd->btnd', branch, w_branch)
    gate = jax.nn.sigmoid(gates.reshape(B, T, N_BRANCH, D_MODEL))
    mix = (gate * proj_b).sum(axis=2) @ w_out
    h = layer_norm(ALPHA * h + mix, ln_g[1], ln_b[1])
    h = layer_norm(ALPHA * h + 0.5 * swiglu(h, up2, down2), ln_g[2], ln_b[2])
    return h, (k_s, v_s, s_new, buf_new)


def setup_inputs(seed: int = 0) -> dict:
    key = jax.random.key(seed)
    ks = jax.random.split(key, 20)
    f32 = jnp.float32
    nrm = lambda k, shape: jax.random.normal(k, shape, f32)
    return {
        'x_prompt': nrm(ks[0], (BATCH, SEQ, D_MODEL)),
        'x_sample': nrm(ks[1], (DEC_BATCH, DEC_SEQ, D_MODEL)),
        'cache_sb_k': nrm(ks[2], (DEPTH, DEC_BATCH, PAST_LEN, H_SB, D_SB)),
        'cache_sb_v': nrm(ks[3], (DEPTH, DEC_BATCH, PAST_LEN, H_SB, D_SB)),
        'state_ret': 0.5 * nrm(ks[4], (DEPTH, DEC_BATCH, H_RET, DK_RET, DV_RET)),
        'state_pool': nrm(ks[5], (DEPTH, DEC_BATCH, POOL_BUF, D_POOL)),
        'meta_tokens': nrm(ks[6], (N_META, D_MODEL)),
        'w_in': nrm(ks[7], (DEPTH, D_MODEL, D_IN)) * D_MODEL ** -0.5,
        'ret_norm_g': 1.0 + 0.1 * nrm(ks[8], (DEPTH, H_RET, DV_RET)),
        'pool_mix_w': nrm(ks[9], (DEPTH, N_POOL_GROUPS, DG_POOL, DG_POOL)) * DG_POOL ** -0.5,
        'pool_scale': 1.0 + 0.1 * nrm(ks[10], (DEPTH, D_POOL)),
        'w_branch': nrm(ks[11], (DEPTH, N_BRANCH, D_BRANCH, D_MODEL)) * (D_BRANCH ** -0.5 * BETA),
        'w_out': nrm(ks[12], (DEPTH, D_MODEL, D_MODEL)) * (D_MODEL ** -0.5 * BETA),
        'ffn1_up': nrm(ks[13], (DEPTH, D_MODEL, 2 * D_FF)) * D_MODEL ** -0.5,
        'ffn1_down': nrm(ks[14], (DEPTH, D_FF, D_MODEL)) * (D_FF ** -0.5 * BETA),
        'ffn2_up': nrm(ks[15], (DEPTH, D_MODEL, 2 * D_FF)) * D_MODEL ** -0.5,
        'ffn2_down': nrm(ks[16], (DEPTH, D_FF, D_MODEL)) * (D_FF ** -0.5 * BETA),
        'ln_g': 1.0 + 0.1 * nrm(ks[17], (DEPTH, 3, D_MODEL)),
        'ln_b': 0.02 * nrm(ks[18], (DEPTH, 3, D_MODEL)),
    }


def reference(x_prompt, x_sample, cache_sb_k, cache_sb_v, state_ret, state_pool, meta_tokens,
              w_in, ret_norm_g, pool_mix_w, pool_scale, w_branch, w_out,
              ffn1_up, ffn1_down, ffn2_up, ffn2_down, ln_g, ln_b):
    dt = meta_tokens.dtype
    B = x_prompt.shape[0]
    bc = lambda a: jnp.broadcast_to(a, (B,) + a.shape[1:])
    h_meta = meta_tokens[None]
    h_p = x_prompt
    h_s = x_sample
    pk_l, pv_l, ps_l, pb_l = [], [], [], []
    sk_l, sv_l, ss_l, sb_l = [], [], [], []
    for l in range(DEPTH):
        lp = (w_in[l], ret_norm_g[l], pool_mix_w[l], pool_scale[l], w_branch[l], w_out[l],
              ffn1_up[l], ffn1_down[l], ffn2_up[l], ffn2_down[l], ln_g[l], ln_b[l])
        empty_kv = jnp.zeros((1, 0, H_SB, D_SB), dt)
        h_meta, (mk, mv, ms, mbuf) = trunk_layer(
            h_meta, empty_kv, empty_kv, jnp.zeros((1, H_RET, DK_RET, DV_RET), dt),
            jnp.zeros((1, 0, D_POOL), dt), 0, lp)
        h_p, (pk, pv, ps, pbuf) = trunk_layer(
            h_p, bc(mk), bc(mv), bc(ms), bc(mbuf), N_META, lp)
        h_s, (sk, sv, ss, sbuf) = trunk_layer(
            h_s, cache_sb_k[l], cache_sb_v[l], state_ret[l], state_pool[l], N_META + PAST_LEN, lp)
        pk_l.append(jnp.concatenate([bc(mk), pk], axis=1))
        pv_l.append(jnp.concatenate([bc(mv), pv], axis=1))
        ps_l.append(ps)
        pb_l.append(pbuf)
        sk_l.append(sk)
        sv_l.append(sv)
        ss_l.append(ss)
        sb_l.append(sbuf)
    new_sb_k_prompt = jnp.stack(pk_l)
    new_sb_v_prompt = jnp.stack(pv_l)
    new_ret_prompt = jnp.stack(ps_l)
    new_pool_prompt = jnp.stack(pb_l)
    new_sb_k_sample = jnp.stack(sk_l)
    new_sb_v_sample = jnp.stack(sv_l)
    new_ret_sample = jnp.stack(ss_l)
    new_pool_sample = jnp.stack(sb_l)
    return (h_p, h_s, new_sb_k_prompt, new_sb_v_prompt, new_ret_prompt, new_pool_prompt,
            new_sb_k_sample, new_sb_v_sample, new_ret_sample, new_pool_sample)
```

```python
import functools
import math

import jax
import jax.numpy as jnp
from jax import lax
from jax.experimental import pallas as pl
from jax.experimental.pallas import tpu as pltpu

D_MODEL = 1024
DEPTH = 2
N_META = 16
H_RET = 4
DK_RET = D_MODEL // 8
DV_RET = 2 * DK_RET
H_SB = 8
D_SB = D_MODEL // 8
RET_CHUNK = 64
POOL_WINDOWS = (2, 4, 8, 16)
DG_POOL = D_MODEL // len(POOL_WINDOWS)
POOL_BUF = max(POOL_WINDOWS) - 1
POOL_HALO = 16
D_FF = ((8 * D_MODEL // 3 + 127) // 128) * 128
D_IN = 10 * D_MODEL
ROPE_BASE = 10000.0
LN_EPS = 1e-5
ALPHA = (2 * DEPTH) ** 0.25

COL_QR, COL_KR, COL_VR, COL_GR = 0, 512, 1024, 2048
COL_QS, COL_KS, COL_VS, COL_U, COL_GATES = 3072, 4096, 5120, 6144, 7168

SB_BLOCK = 128
VMEM_LIMIT = 48 * 1024 * 1024

F32 = jnp.float32
BF16 = jnp.bfloat16


def _params(*sem):
    return pltpu.CompilerParams(dimension_semantics=sem, vmem_limit_bytes=VMEM_LIMIT)


def _layer_norm(y, g, b):
    mu = jnp.mean(y, axis=-1, keepdims=True)
    d = y - mu
    var = jnp.mean(d * d, axis=-1, keepdims=True)
    return d * lax.rsqrt(var + LN_EPS) * g + b


def _silu(x):
    return x * jax.nn.sigmoid(x)


def _ffn_ln_body(x_ref, wg_ref, wu_ref, wd_ref, g_ref, b_ref, *rest, emit_bf16):
    if emit_bf16:
        o_ref, ob_ref, acc_ref, xb_ref = rest
    else:
        o_ref, acc_ref, xb_ref = rest
    j = pl.program_id(1)

    @pl.when(j == 0)
    def _():
        xb_ref[...] = x_ref[...].astype(BF16)
        acc_ref[...] = jnp.zeros_like(acc_ref)

    xb = xb_ref[...]
    gate = jnp.dot(xb, wg_ref[...], preferred_element_type=F32)
    up = jnp.dot(xb, wu_ref[...], preferred_element_type=F32)
    act = (_silu(gate) * up).astype(BF16)
    acc_ref[...] += jnp.dot(act, wd_ref[...], preferred_element_type=F32)

    @pl.when(j == pl.num_programs(1) - 1)
    def _():
        h = _layer_norm(ALPHA * x_ref[...] + 0.5 * acc_ref[...], g_ref[...], b_ref[...])
        o_ref[...] = h
        if emit_bf16:
            ob_ref[...] = h.astype(BF16)


def _ffn_ln(x, w_up, w_down, g, b, emit_bf16):
    m = x.shape[0]
    tm = min(m, 512)
    tf = D_FF // 2
    nf = D_FF // tf
    out_shape = [jax.ShapeDtypeStruct((m, D_MODEL), F32)]
    out_specs = [pl.BlockSpec((tm, D_MODEL), lambda i, j: (i, 0))]
    if emit_bf16:
        out_shape.append(jax.ShapeDtypeStruct((m, D_MODEL), BF16))
        out_specs.append(pl.BlockSpec((tm, D_MODEL), lambda i, j: (i, 0)))
    res = pl.pallas_call(
        functools.partial(_ffn_ln_body, emit_bf16=emit_bf16),
        grid=(m // tm, nf),
        in_specs=[
            pl.BlockSpec((tm, D_MODEL), lambda i, j: (i, 0)),
            pl.BlockSpec((D_MODEL, tf), lambda i, j: (0, j)),
            pl.BlockSpec((D_MODEL, tf), lambda i, j: (0, j + nf)),
            pl.BlockSpec((tf, D_MODEL), lambda i, j: (j, 0)),
            pl.BlockSpec((1, D_MODEL), lambda i, j: (0, 0)),
            pl.BlockSpec((1, D_MODEL), lambda i, j: (0, 0)),
        ],
        out_specs=out_specs,
        out_shape=out_shape,
        scratch_shapes=[pltpu.VMEM((tm, D_MODEL), F32), pltpu.VMEM((tm, D_MODEL), BF16)],
        compiler_params=_params("parallel", "arbitrary"),
        name="ffn_ln",
    )(x, w_up, w_up, w_down, g.reshape(1, D_MODEL), b.reshape(1, D_MODEL))
    return res if emit_bf16 else res[0]


def _matmul_body(x_ref, w_ref, o_ref):
    o_ref[...] = jnp.dot(x_ref[...], w_ref[...], preferred_element_type=F32)


def _in_proj(xb, w_in):
    m = xb.shape[0]
    tm = min(m, 1024)
    tn = 1024
    return pl.pallas_call(
        _matmul_body,
        grid=(m // tm, D_IN // tn),
        in_specs=[
            pl.BlockSpec((tm, D_MODEL), lambda i, j: (i, 0)),
            pl.BlockSpec((D_MODEL, tn), lambda i, j: (0, j)),
        ],
        out_specs=pl.BlockSpec((tm, tn), lambda i, j: (i, j)),
        out_shape=jax.ShapeDtypeStruct((m, D_IN), F32),
        compiler_params=_params("parallel", "arbitrary"),
        name="in_proj",
    )(xb, w_in)


def _retention_body(q_ref, k_ref, v_ref, gr_ref, cos_ref, sin_ref, s0_ref, dintra_ref, dq_ref,
                    dk_ref, dc_ref, rg_ref, o_ref, snew_ref, s_sc, *, chunk, n_chunks):
    s_sc[...] = s0_ref[0, 0]
    d_intra = dintra_ref[0]
    d_q = dq_ref[0]
    d_k = dk_ref[0]
    d_c = dc_ref[0]
    rg = rg_ref[0]

    def rot(x, cos, sin):
        return x * cos + pltpu.roll(x, DK_RET // 2, 1) * sin

    def step(c, carry):
        r0 = pl.multiple_of(c * chunk, chunk)
        rows = pl.ds(r0, chunk)
        cos = cos_ref[rows, :]
        sin = sin_ref[rows, :]
        q = rot(q_ref[0, rows, :], cos, sin)
        k = rot(k_ref[0, rows, :], cos, sin) * (DK_RET ** -0.5)
        qb = q.astype(BF16)
        kb = k.astype(BF16)
        kdb = (k * d_k).astype(BF16)
        vb = v_ref[0, rows, :].astype(BF16)
        att = lax.dot_general(qb, kb, (((1,), (1,)), ((), ())), preferred_element_type=F32) * d_intra
        s = s_sc[...]
        o = (jnp.dot(att.astype(BF16), vb, preferred_element_type=F32)
             + jnp.dot(qb, s.astype(BF16), preferred_element_type=F32) * d_q)
        s_sc[...] = s * d_c + lax.dot_general(kdb, vb, (((0,), (0,)), ((), ())),
                                              preferred_element_type=F32)
        mu = jnp.mean(o, axis=-1, keepdims=True)
        d = o - mu
        var = jnp.mean(d * d, axis=-1, keepdims=True)
        on = d * lax.rsqrt(var + LN_EPS) * rg
        o_ref[0, rows, :] = (on * _silu(gr_ref[0, rows, :])).astype(o_ref.dtype)
        return carry

    lax.fori_loop(0, n_chunks, step, 0)
    snew_ref[0, 0] = s_sc[...]


def _retention_decays(chunk):
    lg = jnp.log1p(-jnp.exp(jnp.linspace(math.log(1.0 / 32), math.log(1.0 / 512), H_RET, dtype=F32)))
    i = jnp.arange(chunk, dtype=F32)
    diff = i[:, None] - i[None, :]
    d_intra = jnp.where(diff >= 0, jnp.exp(lg[:, None, None] * jnp.maximum(diff, 0.0)), 0.0)
    d_q = jnp.exp(lg[:, None] * (i[None, :] + 1.0))
    d_k = jnp.exp(lg[:, None] * (chunk - 1.0 - i[None, :]))
    d_c = jnp.exp(lg * chunk)
    return (d_intra,
            jnp.broadcast_to(d_q[:, :, None], (H_RET, chunk, DV_RET)),
            jnp.broadcast_to(d_k[:, :, None], (H_RET, chunk, DK_RET)),
            jnp.broadcast_to(d_c[:, None, None], (H_RET, 1, DV_RET)))


def _rotary_tables(pos0, t):
    half = DK_RET // 2
    inv_freq = ROPE_BASE ** (-jnp.arange(half, dtype=F32) / half)
    ang = (pos0 + jnp.arange(t, dtype=F32))[:, None] * inv_freq[None, :]
    cos = jnp.cos(ang)
    sin = jnp.sin(ang)
    return jnp.concatenate([cos, cos], axis=-1), jnp.concatenate([-sin, sin], axis=-1)


def _retention(proj3, pos0, s0, ret_g):
    bsz, t, _ = proj3.shape
    chunk = min(RET_CHUNK, t)
    n_chunks = t // chunk
    cos, sin = _rotary_tables(pos0, t)
    d_intra, d_q, d_k, d_c = _retention_decays(chunk)
    s0_map = (lambda b, h: (b, h, 0, 0)) if s0.shape[0] == bsz else (lambda b, h: (0, h, 0, 0))
    per_head = lambda b, h: (h, 0, 0)
    return pl.pallas_call(
        functools.partial(_retention_body, chunk=chunk, n_chunks=n_chunks),
        grid=(bsz, H_RET),
        in_specs=[
            pl.BlockSpec((1, t, DK_RET), lambda b, h: (b, 0, COL_QR // DK_RET + h)),
            pl.BlockSpec((1, t, DK_RET), lambda b, h: (b, 0, COL_KR // DK_RET + h)),
            pl.BlockSpec((1, t, DV_RET), lambda b, h: (b, 0, COL_VR // DV_RET + h)),
            pl.BlockSpec((1, t, DV_RET), lambda b, h: (b, 0, COL_GR // DV_RET + h)),
            pl.BlockSpec((t, DK_RET), lambda b, h: (0, 0)),
            pl.BlockSpec((t, DK_RET), lambda b, h: (0, 0)),
            pl.BlockSpec((1, 1, DK_RET, DV_RET), s0_map),
            pl.BlockSpec((1, chunk, chunk), per_head),
            pl.BlockSpec((1, chunk, DV_RET), per_head),
            pl.BlockSpec((1, chunk, DK_RET), per_head),
            pl.BlockSpec((1, 1, DV_RET), per_head),
            pl.BlockSpec((1, 1, DV_RET), per_head),
        ],
        out_specs=[
            pl.BlockSpec((1, t, DV_RET), lambda b, h: (b, 0, h)),
            pl.BlockSpec((1, 1, DK_RET, DV_RET), lambda b, h: (b, h, 0, 0)),
        ],
        out_shape=[
            jax.ShapeDtypeStruct((bsz, t, H_RET * DV_RET), BF16),
            jax.ShapeDtypeStruct((bsz, H_RET, DK_RET, DV_RET), F32),
        ],
        scratch_shapes=[pltpu.VMEM((DK_RET, DV_RET), F32)],
        compiler_params=_params("parallel", "parallel"),
        name="retention",
    )(proj3, proj3, proj3, proj3, cos, sin, s0, d_intra, d_q, d_k, d_c,
      ret_g.reshape(H_RET, 1, DV_RET))


def _stick_breaking_body(q_ref, kn_ref, vn_ref, kp_ref, vp_ref, o_ref, kn_sc, vn_sc, kp_sc, vp_sc,
                         *, qb, n_past_valid):
    kb = SB_BLOCK
    i = pl.program_id(2)

    @pl.when(i == 0)
    def _():
        kn_sc[...] = kn_ref[0].astype(BF16)
        vn_sc[...] = vn_ref[0].astype(BF16)
        kp_sc[...] = kp_ref[0].astype(BF16)
        vp_sc[...] = vp_ref[0].astype(BF16)

    q = q_ref[0].astype(BF16)
    scale = D_SB ** -0.5
    row = lax.broadcasted_iota(jnp.int32, (qb, kb), 0)
    col = lax.broadcasted_iota(jnp.int32, (qb, kb), 1)
    jj = lax.broadcasted_iota(jnp.int32, (kb, 2 * kb), 0)
    ss = lax.broadcasted_iota(jnp.int32, (kb, 2 * kb), 1)
    suffix_ones = jnp.where((ss >= kb) | (jj >= ss), 1.0, 0.0).astype(BF16)

    def tile(k_blk, v_blk, mask, carry, acc):
        z = lax.dot_general(q, k_blk, (((1,), (1,)), ((), ())), preferred_element_type=F32) * scale
        log_stay = -(jnp.maximum(z, 0.0) + jnp.log1p(jnp.exp(-jnp.abs(z))))
        if mask is not None:
            log_stay = jnp.where(mask, log_stay, 0.0)
        hi = log_stay.astype(BF16)
        lo = (log_stay - hi.astype(F32)).astype(BF16)
        sums = jnp.dot(jnp.concatenate([hi, lo], axis=0), suffix_ones, preferred_element_type=F32)
        sums = sums[:qb] + sums[qb:]
        w = jnp.exp(z + sums[:, :kb] + carry)
        if mask is not None:
            w = jnp.where(mask, w, 0.0)
        acc = acc + jnp.dot(w.astype(BF16), v_blk, preferred_element_type=F32)
        return carry + sums[:, kb:], acc

    carry = jnp.zeros((qb, kb), F32)
    acc = jnp.zeros((qb, D_SB), F32)

    d0 = pl.multiple_of(i * kb, kb) if qb == kb else 0
    carry, acc = tile(kn_sc[pl.ds(d0, kb), :], vn_sc[pl.ds(d0, kb), :], col < row, carry, acc)

    def earlier(t, ca):
        j0 = pl.multiple_of((i - 1 - t) * kb, kb)
        return tile(kn_sc[pl.ds(j0, kb), :], vn_sc[pl.ds(j0, kb), :], None, *ca)

    carry, acc = lax.fori_loop(0, i, earlier, (carry, acc))

    n_full, rem = divmod(n_past_valid, kb)
    if rem:
        p0 = n_full * kb
        carry, acc = tile(kp_sc[p0:p0 + kb, :], vp_sc[p0:p0 + kb, :], col < rem, carry, acc)
    if n_full:
        def cached(t, ca):
            j0 = pl.multiple_of((n_full - 1 - t) * kb, kb)
            return tile(kp_sc[pl.ds(j0, kb), :], vp_sc[pl.ds(j0, kb), :], None, *ca)

        carry, acc = lax.fori_loop(0, n_full, cached, (carry, acc))

    o_ref[0] = acc.astype(o_ref.dtype)


def _stick_breaking(q_arr, q_col, kv_arr, k_col, v_col, past_k, past_v, n_past_valid):
    bsz, t, _ = q_arr.shape
    tk = kv_arr.shape[1]
    p = past_k.shape[1]
    qb = min(SB_BLOCK, t)
    assert t % qb == 0 and tk % SB_BLOCK == 0 and p % SB_BLOCK == 0 and tk >= t
    assert qb == SB_BLOCK or t == qb
    past_map = ((lambda b, h, i: (b, 0, h)) if past_k.shape[0] == bsz else (lambda b, h, i: (0, 0, h)))
    return pl.pallas_call(
        functools.partial(_stick_breaking_body, qb=qb, n_past_valid=n_past_valid),
        grid=(bsz, H_SB, t // qb),
        in_specs=[
            pl.BlockSpec((1, qb, D_SB), lambda b, h, i: (b, i, q_col // D_SB + h)),
            pl.BlockSpec((1, tk, D_SB), lambda b, h, i: (b, 0, k_col // D_SB + h)),
            pl.BlockSpec((1, tk, D_SB), lambda b, h, i: (b, 0, v_col // D_SB + h)),
            pl.BlockSpec((1, p, D_SB), past_map),
            pl.BlockSpec((1, p, D_SB), past_map),
        ],
        out_specs=pl.BlockSpec((1, qb, D_SB), lambda b, h, i: (b, i, h)),
        out_shape=jax.ShapeDtypeStruct((bsz, t, H_SB * D_SB), BF16),
        scratch_shapes=[pltpu.VMEM((tk, D_SB), BF16), pltpu.VMEM((tk, D_SB), BF16),
                        pltpu.VMEM((p, D_SB), BF16), pltpu.VMEM((p, D_SB), BF16)],
        compiler_params=_params("parallel", "parallel", "arbitrary"),
        name="stick_breaking",
    )(q_arr, kv_arr, kv_arr, past_k, past_v)


def _pool_body(u_ref, prev_ref, buf_ref, w_ref, sc_ref, o_ref, z_sc, *, tm, n_hist):
    i = pl.program_id(1)
    z_sc[POOL_HALO:POOL_HALO + tm, :] = u_ref[0]

    @pl.when(i == 0)
    def _():
        z_sc[0:POOL_HALO, :] = buf_ref[0]

    @pl.when(i > 0)
    def _():
        z_sc[0:POOL_HALO, :] = prev_ref[0]

    tr = min(tm, 256)
    for g, win in enumerate(POOL_WINDOWS):
        cols = slice(g * DG_POOL, (g + 1) * DG_POOL)
        for r0 in range(0, tm, tr):
            base = POOL_HALO + r0
            wsum = z_sc[base:base + tr, cols]
            for back in range(1, win):
                wsum = wsum + z_sc[base - back:base - back + tr, cols]
            t_idx = i * tm + r0 + lax.broadcasted_iota(jnp.int32, (tr, 1), 0)
            cnt = jnp.minimum(t_idx + (n_hist + 1), win).astype(F32)
            pooled = wsum / cnt - z_sc[base:base + tr, cols]
            mixed = jnp.dot(pooled.astype(BF16), w_ref[g], preferred_element_type=F32)
            o_ref[0, r0:r0 + tr, cols] = (mixed * sc_ref[:, cols]).astype(o_ref.dtype)


def _pool_mixer(proj3, hist, n_hist, mix_w, scale):
    bsz, t, _ = proj3.shape
    tm = min(t, 512)
    hpb = tm // POOL_HALO
    ucol = COL_U // D_MODEL
    hist_map = (lambda b, i: (b, 0, 0)) if hist.shape[0] == bsz else (lambda b, i: (0, 0, 0))
    return pl.pallas_call(
        functools.partial(_pool_body, tm=tm, n_hist=n_hist),
        grid=(bsz, t // tm),
        in_specs=[
            pl.BlockSpec((1, tm, D_MODEL), lambda b, i: (b, i, ucol)),
            pl.BlockSpec((1, POOL_HALO, D_MODEL),
                         lambda b, i: (b, jnp.maximum(i * hpb - 1, 0), ucol)),
            pl.BlockSpec((1, POOL_HALO, D_MODEL), hist_map),
            pl.BlockSpec((len(POOL_WINDOWS), DG_POOL, DG_POOL), lambda b, i: (0, 0, 0)),
            pl.BlockSpec((1, D_MODEL), lambda b, i: (0, 0)),
        ],
        out_specs=pl.BlockSpec((1, tm, D_MODEL), lambda b, i: (b, i, 0)),
        out_shape=jax.ShapeDtypeStruct((bsz, t, D_MODEL), BF16),
        scratch_shapes=[pltpu.VMEM((POOL_HALO + tm, D_MODEL), F32)],
        compiler_params=_params("parallel", "arbitrary"),
        name="pool_mixer",
    )(proj3, proj3, hist, mix_w, scale.reshape(1, D_MODEL))


def _merge_body(or_ref, os_ref, op_ref, g0_ref, g1_ref, g2_ref, h_ref, wb_ref, wo_ref, g_ref, b_ref,
                o_ref):
    mix = jax.nn.sigmoid(g0_ref[...]) * jnp.dot(or_ref[...], wb_ref[0], preferred_element_type=F32)
    mix = mix + jax.nn.sigmoid(g1_ref[...]) * jnp.dot(os_ref[...], wb_ref[1],
                                                      preferred_element_type=F32)
    mix = mix + jax.nn.sigmoid(g2_ref[...]) * jnp.dot(op_ref[...], wb_ref[2],
                                                      preferred_element_type=F32)
    out = jnp.dot(mix.astype(BF16), wo_ref[...], preferred_element_type=F32)
    o_ref[...] = _layer_norm(ALPHA * h_ref[...] + out, g_ref[...], b_ref[...])


def _merge(o_r, o_s, o_p, proj, h, w_branch, w_out, g, b):
    m = h.shape[0]
    tm = min(m, 512)
    row = lambda i: (i, 0)
    gcol = COL_GATES // D_MODEL
    return pl.pallas_call(
        _merge_body,
        grid=(m // tm,),
        in_specs=[
            pl.BlockSpec((tm, D_MODEL), row),
            pl.BlockSpec((tm, D_MODEL), row),
            pl.BlockSpec((tm, D_MODEL), row),
            pl.BlockSpec((tm, D_MODEL), lambda i: (i, gcol)),
            pl.BlockSpec((tm, D_MODEL), lambda i: (i, gcol + 1)),
            pl.BlockSpec((tm, D_MODEL), lambda i: (i, gcol + 2)),
            pl.BlockSpec((tm, D_MODEL), row),
            pl.BlockSpec((3, D_MODEL, D_MODEL), lambda i: (0, 0, 0)),
            pl.BlockSpec((D_MODEL, D_MODEL), lambda i: (0, 0)),
            pl.BlockSpec((1, D_MODEL), lambda i: (0, 0)),
            pl.BlockSpec((1, D_MODEL), lambda i: (0, 0)),
        ],
        out_specs=pl.BlockSpec((tm, D_MODEL), row),
        out_shape=jax.ShapeDtypeStruct((m, D_MODEL), F32),
        compiler_params=_params("parallel"),
        name="merge",
    )(o_r, o_s, o_p, proj, proj, proj, h, w_branch, w_out, g.reshape(1, D_MODEL),
      b.reshape(1, D_MODEL))


def _pad_rows(a, rows):
    return jnp.pad(a, ((0, 0), (0, rows - a.shape[1]), (0, 0)))


def _trunk_layer(h, past_k, past_v, s0, pool_buf, pos0, lw):
    bsz, t, _ = h.shape
    m = bsz * t
    h1, h1b = _ffn_ln(h.reshape(m, D_MODEL), lw["up1"], lw["down1"], lw["ln_g"][0], lw["ln_b"][0],
                      emit_bf16=True)
    proj = _in_proj(h1b, lw["w_in"])
    proj3 = proj.reshape(bsz, t, D_IN)

    o_r, s_new = _retention(proj3, pos0, s0, lw["ret_g"])

    n_past = past_k.shape[1]
    p_rows = max(SB_BLOCK, -(-n_past // SB_BLOCK) * SB_BLOCK)
    past_k = _pad_rows(past_k, p_rows)
    past_v = _pad_rows(past_v, p_rows)
    if t % SB_BLOCK == 0:
        o_s = _stick_breaking(proj3, COL_QS, proj3, COL_KS, COL_VS, past_k, past_v, n_past)
    else:
        kv = _pad_rows(proj3[:, :, COL_KS:COL_VS + D_MODEL], SB_BLOCK)
        o_s = _stick_breaking(proj3, COL_QS, kv, 0, D_MODEL, past_k, past_v, n_past)

    n_hist = pool_buf.shape[1]
    hist = jnp.pad(pool_buf, ((0, 0), (POOL_HALO - n_hist, 0), (0, 0)))
    o_p = _pool_mixer(proj3, hist, n_hist, lw["pool_w"], lw["pool_scale"])

    h2 = _merge(o_r.reshape(m, D_MODEL), o_s.reshape(m, D_MODEL), o_p.reshape(m, D_MODEL), proj, h1,
                lw["w_branch"], lw["w_out"], lw["ln_g"][1], lw["ln_b"][1])
    h3 = _ffn_ln(h2, lw["up2"], lw["down2"], lw["ln_g"][2], lw["ln_b"][2], emit_bf16=False)

    k_s = proj3[:, :, COL_KS:COL_KS + D_MODEL]
    v_s = proj3[:, :, COL_VS:COL_VS + D_MODEL]
    u = proj3[:, :, COL_U:COL_U + D_MODEL]
    assert t >= POOL_BUF
    buf_new = u[:, t - POOL_BUF:]
    return h3.reshape(bsz, t, D_MODEL), (k_s, v_s, s_new, buf_new)


def _layer_weights(l, w_in, ret_norm_g, pool_mix_w, pool_scale, w_branch, w_out,
                   ffn1_up, ffn1_down, ffn2_up, ffn2_down, ln_g, ln_b):
    return dict(
        w_in=w_in[l].astype(BF16), ret_g=ret_norm_g[l], pool_w=pool_mix_w[l].astype(BF16),
        pool_scale=pool_scale[l], w_branch=w_branch[l].astype(BF16), w_out=w_out[l].astype(BF16),
        up1=ffn1_up[l].astype(BF16), down1=ffn1_down[l].astype(BF16),
        up2=ffn2_up[l].astype(BF16), down2=ffn2_down[l].astype(BF16),
        ln_g=ln_g[l], ln_b=ln_b[l])


def kernel(x_prompt, x_sample, cache_sb_k, cache_sb_v, state_ret, state_pool, meta_tokens, w_in, ret_norm_g, pool_mix_w, pool_scale, w_branch, w_out, ffn1_up, ffn1_down, ffn2_up, ffn2_down, ln_g, ln_b):
    bsz = x_prompt.shape[0]
    dec_b, past_len = cache_sb_k.shape[1], cache_sb_k.shape[2]
    h_meta = meta_tokens[None]
    h_p = x_prompt
    h_s = x_sample
    outs = [[] for _ in range(8)]
    for l in range(DEPTH):
        lw = _layer_weights(l, w_in, ret_norm_g, pool_mix_w, pool_scale, w_branch, w_out,
                            ffn1_up, ffn1_down, ffn2_up, ffn2_down, ln_g, ln_b)
        empty_kv = jnp.zeros((1, 0, D_MODEL), F32)
        h_meta, (mk, mv, ms, mbuf) = _trunk_layer(
            h_meta, empty_kv, empty_kv, jnp.zeros((1, H_RET, DK_RET, DV_RET), F32),
            jnp.zeros((1, 0, D_MODEL), F32), 0, lw)
        h_p, (pk, pv, ps, pbuf) = _trunk_layer(h_p, mk, mv, ms, mbuf, N_META, lw)
        h_s, (sk, sv, ss, sbuf) = _trunk_layer(
            h_s, cache_sb_k[l].reshape(dec_b, past_len, D_MODEL),
            cache_sb_v[l].reshape(dec_b, past_len, D_MODEL), state_ret[l], state_pool[l],
            N_META + past_len, lw)
        bc = lambda a: jnp.broadcast_to(a, (bsz,) + a.shape[1:])
        heads = lambda a: a.reshape(a.shape[0], a.shape[1], H_SB, D_SB)
        outs[0].append(heads(jnp.concatenate([bc(mk), pk], axis=1)))
        outs[1].append(heads(jnp.concatenate([bc(mv), pv], axis=1)))
        outs[2].append(ps)
        outs[3].append(pbuf)
        outs[4].append(heads(sk))
        outs[5].append(heads(sv))
        outs[6].append(ss)
        outs[7].append(sbuf)
    return (h_p, h_s) + tuple(jnp.stack(o) for o in outs)
```

```python
import functools
import math

import jax
import jax.numpy as jnp
from jax import lax
from jax.experimental import pallas as pl
from jax.experimental.pallas import tpu as pltpu

D_MODEL = 1024
DEPTH = 2
N_META = 16
H_RET = 4
DK_RET = D_MODEL // 8
DV_RET = 2 * DK_RET
H_SB = 8
D_SB = D_MODEL // 8
RET_CHUNK = 256
POOL_WINDOWS = (2, 4, 8, 16)
DG_POOL = D_MODEL // len(POOL_WINDOWS)
POOL_BUF = max(POOL_WINDOWS) - 1
POOL_HALO = 16
D_FF = ((8 * D_MODEL // 3 + 127) // 128) * 128
D_IN = 10 * D_MODEL
ROPE_BASE = 10000.0
LN_EPS = 1e-5
LOG2E = 1.0 / math.log(2.0)
ALPHA = (2 * DEPTH) ** 0.25

W_COL_KS, W_COL_VS = 4096, 5120
D_PROJ = D_IN - 2 * D_MODEL
COL_QR, COL_KR, COL_VR, COL_GR = 0, 512, 1024, 2048
COL_QS, COL_U, COL_GATES = 3072, 4096, 5120

SB_BLOCK = 128
VMEM_LIMIT = 48 * 1024 * 1024

F32 = jnp.float32
BF16 = jnp.bfloat16


def _params(*sem):
    return pltpu.CompilerParams(dimension_semantics=sem, vmem_limit_bytes=VMEM_LIMIT)


def _layer_norm(y, g, b):
    mu = jnp.mean(y, axis=-1, keepdims=True)
    d = y - mu
    var = jnp.mean(d * d, axis=-1, keepdims=True)
    return d * lax.rsqrt(var + LN_EPS) * g + b


def _silu(x):
    return x * jax.nn.sigmoid(x)


def _ffn_ln_body(x_ref, wg_ref, wu_ref, wd_ref, g_ref, b_ref, *rest, emit_bf16):
    if emit_bf16:
        o_ref, ob_ref, acc_ref, xb_ref = rest
    else:
        o_ref, acc_ref, xb_ref = rest
    j = pl.program_id(1)

    @pl.when(j == 0)
    def _():
        xb_ref[...] = x_ref[...].astype(BF16)
        acc_ref[...] = jnp.zeros_like(acc_ref)

    xb = xb_ref[...]
    gate = jnp.dot(xb, wg_ref[...], preferred_element_type=F32)
    up = jnp.dot(xb, wu_ref[...], preferred_element_type=F32)
    act = (_silu(gate) * up).astype(BF16)
    acc_ref[...] += jnp.dot(act, wd_ref[...], preferred_element_type=F32)

    @pl.when(j == pl.num_programs(1) - 1)
    def _():
        h = _layer_norm(ALPHA * x_ref[...] + 0.5 * acc_ref[...], g_ref[...], b_ref[...])
        o_ref[...] = h
        if emit_bf16:
            ob_ref[...] = h.astype(BF16)


def _ffn_ln(x, w_up, w_down, g, b, emit_bf16):
    m = x.shape[0]
    tm = min(m, 512)
    tf = D_FF // 2
    nf = D_FF // tf
    out_shape = [jax.ShapeDtypeStruct((m, D_MODEL), F32)]
    out_specs = [pl.BlockSpec((tm, D_MODEL), lambda i, j: (i, 0))]
    if emit_bf16:
        out_shape.append(jax.ShapeDtypeStruct((m, D_MODEL), BF16))
        out_specs.append(pl.BlockSpec((tm, D_MODEL), lambda i, j: (i, 0)))
    res = pl.pallas_call(
        functools.partial(_ffn_ln_body, emit_bf16=emit_bf16),
        grid=(m // tm, nf),
        in_specs=[
            pl.BlockSpec((tm, D_MODEL), lambda i, j: (i, 0)),
            pl.BlockSpec((D_MODEL, tf), lambda i, j: (0, j)),
            pl.BlockSpec((D_MODEL, tf), lambda i, j: (0, j + nf)),
            pl.BlockSpec((tf, D_MODEL), lambda i, j: (j, 0)),
            pl.BlockSpec((1, D_MODEL), lambda i, j: (0, 0)),
            pl.BlockSpec((1, D_MODEL), lambda i, j: (0, 0)),
        ],
        out_specs=out_specs,
        out_shape=out_shape,
        scratch_shapes=[pltpu.VMEM((tm, D_MODEL), F32), pltpu.VMEM((tm, D_MODEL), BF16)],
        compiler_params=_params("parallel", "arbitrary"),
        name="ffn_ln",
    )(x, w_up, w_up, w_down, g.reshape(1, D_MODEL), b.reshape(1, D_MODEL))
    return res if emit_bf16 else res[0]


def _matmul_body(x_ref, w_ref, o_ref):
    o_ref[...] = jnp.dot(x_ref[...], w_ref[...], preferred_element_type=F32)


def _in_proj(xb, w_in):
    m = xb.shape[0]
    tm = min(m, 1024)
    tn = D_MODEL
    assert m % tm == 0
    skip_from, n_skip = W_COL_KS // tn, 2 * D_MODEL // tn
    return pl.pallas_call(
        _matmul_body,
        grid=(m // tm, D_PROJ // tn),
        in_specs=[
            pl.BlockSpec((tm, D_MODEL), lambda i, j: (i, 0)),
            pl.BlockSpec((D_MODEL, tn), lambda i, j: (0, jnp.where(j < skip_from, j, j + n_skip))),
        ],
        out_specs=pl.BlockSpec((tm, tn), lambda i, j: (i, j)),
        out_shape=jax.ShapeDtypeStruct((m, D_PROJ), F32),
        compiler_params=_params("parallel", "arbitrary"),
        name="in_proj",
    )(xb, w_in)


def _kv_body(x_ref, wk_ref, wv_ref, kf_ref, vf_ref, kb_ref, vb_ref):
    x = x_ref[...]
    k = jnp.dot(x, wk_ref[...], preferred_element_type=F32)
    v = jnp.dot(x, wv_ref[...], preferred_element_type=F32)
    kf_ref[0] = k
    vf_ref[0] = v
    kb_ref[0] = k.astype(BF16)
    vb_ref[0] = v.astype(BF16)


def _kv_proj(xb, w_in, bsz, t, row0):
    tm = min(t, 512)
    assert t % tm == 0 and row0 % 8 == 0
    nt = t // tm
    half = pl.BlockSpec((1, tm, D_MODEL), lambda b, i: (b, i, 0))
    if row0:
        full = pl.BlockSpec((pl.Element(1), pl.Element(tm), pl.Element(D_MODEL)),
                            lambda b, i: (b, pl.multiple_of(row0 + i * tm, 8), 0))
    else:
        full = half
    return pl.pallas_call(
        _kv_body,
        grid=(bsz, nt),
        in_specs=[
            pl.BlockSpec((tm, D_MODEL), lambda b, i: (b * nt + i, 0)),
            pl.BlockSpec((D_MODEL, D_MODEL), lambda b, i: (0, W_COL_KS // D_MODEL)),
            pl.BlockSpec((D_MODEL, D_MODEL), lambda b, i: (0, W_COL_VS // D_MODEL)),
        ],
        out_specs=[full, full, half, half],
        out_shape=[jax.ShapeDtypeStruct((bsz, row0 + t, D_MODEL), F32)] * 2
        + [jax.ShapeDtypeStruct((bsz, t, D_MODEL), BF16)] * 2,
        compiler_params=_params("parallel", "arbitrary"),
        name="kv_proj",
    )(xb, w_in, w_in)


def _retention_body(q_ref, k_ref, v_ref, gr_ref, cos_ref, sin_ref, s0_ref, dintra_ref, dq_ref,
                    dk_ref, dc_ref, rg_ref, o_ref, snew_ref, qr_sc, kr_sc, kd_sc, vb_sc, s_sc,
                    *, chunk, n_chunks):
    d_intra = dintra_ref[0]
    d_q = dq_ref[0]
    d_k = dk_ref[0]
    d_c = dc_ref[0]
    rg = rg_ref[0]

    def rot(x, cos, sin):
        return x * cos + pltpu.roll(x, DK_RET // 2, 1) * sin

    for c in range(n_chunks):
        rows = slice(c * chunk, (c + 1) * chunk)
        cos = cos_ref[rows, :]
        sin = sin_ref[rows, :]
        k = rot(k_ref[0, rows, :], cos, sin) * (DK_RET ** -0.5)
        qr_sc[rows, :] = rot(q_ref[0, rows, :], cos, sin).astype(BF16)
        kr_sc[rows, :] = k.astype(BF16)
        kd_sc[rows, :] = (k * d_k).astype(BF16)
        vb_sc[rows, :] = v_ref[0, rows, :].astype(BF16)

    s = s0_ref[0, 0]
    for c in range(n_chunks):
        rows = slice(c * chunk, (c + 1) * chunk)
        s_sc[c] = s.astype(BF16)
        s = s * d_c + lax.dot_general(kd_sc[rows, :], vb_sc[rows, :], (((0,), (0,)), ((), ())),
                                      preferred_element_type=F32)
    snew_ref[0, 0] = s

    def products(c):
        rows = slice(c * chunk, (c + 1) * chunk)
        qc = qr_sc[rows, :]
        att = lax.dot_general(qc, kr_sc[rows, :], (((1,), (1,)), ((), ())),
                              preferred_element_type=F32)
        return att, jnp.dot(qc, s_sc[c], preferred_element_type=F32)

    nxt = products(0)
    for c in range(n_chunks):
        rows = slice(c * chunk, (c + 1) * chunk)
        att, inter = nxt
        if c + 1 < n_chunks:
            nxt = products(c + 1)
        o = (jnp.dot((att * d_intra).astype(BF16), vb_sc[rows, :], preferred_element_type=F32)
             + inter * d_q)
        mu = jnp.mean(o, axis=-1, keepdims=True)
        d = o - mu
        var = jnp.mean(d * d, axis=-1, keepdims=True)
        on = d * lax.rsqrt(var + LN_EPS) * rg
        o_ref[0, rows, :] = (on * _silu(gr_ref[0, rows, :])).astype(o_ref.dtype)


def _retention_decays(chunk):
    lg = jnp.log1p(-jnp.exp(jnp.linspace(math.log(1.0 / 32), math.log(1.0 / 512), H_RET, dtype=F32)))
    i = jnp.arange(chunk, dtype=F32)
    diff = i[:, None] - i[None, :]
    d_intra = jnp.where(diff >= 0, jnp.exp(lg[:, None, None] * jnp.maximum(diff, 0.0)), 0.0)
    d_q = jnp.exp(lg[:, None] * (i[None, :] + 1.0))
    d_k = jnp.exp(lg[:, None] * (chunk - 1.0 - i[None, :]))
    d_c = jnp.exp(lg * chunk)
    return (d_intra,
            jnp.broadcast_to(d_q[:, :, None], (H_RET, chunk, DV_RET)),
            jnp.broadcast_to(d_k[:, :, None], (H_RET, chunk, DK_RET)),
            jnp.broadcast_to(d_c[:, None, None], (H_RET, 1, DV_RET)))


def _rotary_tables(pos0, t):
    half = DK_RET // 2
    inv_freq = ROPE_BASE ** (-jnp.arange(half, dtype=F32) / half)
    ang = (pos0 + jnp.arange(t, dtype=F32))[:, None] * inv_freq[None, :]
    cos = jnp.cos(ang)
    sin = jnp.sin(ang)
    return jnp.concatenate([cos, cos], axis=-1), jnp.concatenate([-sin, sin], axis=-1)


def _retention(proj3, pos0, s0, ret_g):
    bsz, t, _ = proj3.shape
    chunk = min(RET_CHUNK, t)
    n_chunks = t // chunk
    cos, sin = _rotary_tables(pos0, t)
    d_intra, d_q, d_k, d_c = _retention_decays(chunk)
    s0_map = (lambda b, h: (b, h, 0, 0)) if s0.shape[0] == bsz else (lambda b, h: (0, h, 0, 0))
    per_head = lambda b, h: (h, 0, 0)
    return pl.pallas_call(
        functools.partial(_retention_body, chunk=chunk, n_chunks=n_chunks),
        grid=(bsz, H_RET),
        in_specs=[
            pl.BlockSpec((1, t, DK_RET), lambda b, h: (b, 0, COL_QR // DK_RET + h)),
            pl.BlockSpec((1, t, DK_RET), lambda b, h: (b, 0, COL_KR // DK_RET + h)),
            pl.BlockSpec((1, t, DV_RET), lambda b, h: (b, 0, COL_VR // DV_RET + h)),
            pl.BlockSpec((1, t, DV_RET), lambda b, h: (b, 0, COL_GR // DV_RET + h)),
            pl.BlockSpec((t, DK_RET), lambda b, h: (0, 0)),
            pl.BlockSpec((t, DK_RET), lambda b, h: (0, 0)),
            pl.BlockSpec((1, 1, DK_RET, DV_RET), s0_map),
            pl.BlockSpec((1, chunk, chunk), per_head),
            pl.BlockSpec((1, chunk, DV_RET), per_head),
            pl.BlockSpec((1, chunk, DK_RET), per_head),
            pl.BlockSpec((1, 1, DV_RET), per_head),
            pl.BlockSpec((1, 1, DV_RET), per_head),
        ],
        out_specs=[
            pl.BlockSpec((1, t, DV_RET), lambda b, h: (b, 0, h)),
            pl.BlockSpec((1, 1, DK_RET, DV_RET), lambda b, h: (b, h, 0, 0)),
        ],
        out_shape=[
            jax.ShapeDtypeStruct((bsz, t, H_RET * DV_RET), BF16),
            jax.ShapeDtypeStruct((bsz, H_RET, DK_RET, DV_RET), F32),
        ],
        scratch_shapes=[pltpu.VMEM((t, DK_RET), BF16), pltpu.VMEM((t, DK_RET), BF16),
                        pltpu.VMEM((t, DK_RET), BF16), pltpu.VMEM((t, DV_RET), BF16),
                        pltpu.VMEM((n_chunks, DK_RET, DV_RET), BF16)],
        compiler_params=_params("parallel", "parallel"),
        name="retention",
    )(proj3, proj3, proj3, proj3, cos, sin, s0, d_intra, d_q, d_k, d_c,
      ret_g.reshape(H_RET, 1, DV_RET))


def _stick_breaking_body(itab_ref, gtab_ref, q_ref, kn_ref, vn_ref, kp_ref, vp_ref, o_ref,
                         qt_sc, kn_sc, vnt_sc, kp_sc, vpt_sc, kr_sc, vrt_sc, acc_sc, carry_sc,
                         z_sc, hilo_sc, incl_sc, *, qb, n_q, n_past_valid):
    kb = SB_BLOCK
    pg = 2 * kb
    n_pfull = n_past_valid // pg
    n_prem = n_past_valid - n_pfull * pg
    n_prem_blocks = -(-n_prem // kb)
    scale = D_SB ** -0.5

    kn_sc[...] = kn_ref[0].astype(BF16)
    for g in range(n_q):
        rows = slice(g * qb, (g + 1) * qb)
        vnt_sc[g] = vn_ref[0, rows, :].T.astype(BF16)
        qt_sc[g] = q_ref[0, rows, :].T.astype(BF16)
    if n_pfull:
        kp_sc[...] = kp_ref[0, 0:n_pfull * pg, :].astype(BF16)
        for g in range(n_pfull):
            vpt_sc[g] = vp_ref[0, g * pg:(g + 1) * pg, :].T.astype(BF16)
    for j in range(n_prem_blocks):
        rows = slice(n_pfull * pg + j * kb, n_pfull * pg + (j + 1) * kb)
        kr_sc[j] = kp_ref[0, rows, :].astype(BF16)
        vrt_sc[j] = vp_ref[0, rows, :].T.astype(BF16)
    acc_sc[...] = jnp.zeros_like(acc_sc)
    carry_sc[...] = jnp.zeros_like(carry_sc)

    suffix_ones = (lax.broadcasted_iota(jnp.int32, (kb, kb), 1)
                   >= lax.broadcasted_iota(jnp.int32, (kb, kb), 0)).astype(BF16)
    suffix_ones2 = jnp.concatenate([suffix_ones, suffix_ones], axis=1)

    def stage_a(k_grp, q_t):
        return jnp.dot(k_grp, q_t, preferred_element_type=F32) * (scale * LOG2E)

    def stage_b(z, mask):
        neg_abs = lax.bitcast_convert_type(
            lax.bitcast_convert_type(z, jnp.uint32) | jnp.uint32(0x80000000), F32)
        go = jnp.maximum(z, 0.0) + jnp.log(1.0 + jnp.exp2(neg_abs)) * LOG2E
        if mask is not None:
            go = jnp.where(mask, go, 0.0)
        hi = go.astype(BF16)
        lo = (go - hi.astype(F32)).astype(BF16)
        parts = []
        for s in range(z.shape[0] // kb):
            parts += [hi[s * kb:(s + 1) * kb], lo[s * kb:(s + 1) * kb]]
        return jnp.concatenate(parts, axis=0)

    def stage_c(hilo):
        parts = [jnp.dot(suffix_ones2, hilo[s * 2 * kb:(s + 1) * 2 * kb], preferred_element_type=F32)
                 for s in range(hilo.shape[0] // (2 * kb))]
        return parts[0] if len(parts) == 1 else jnp.concatenate(parts, axis=0)

    def stage_de(z, incl, mask, vt_grp, qi):
        carry = carry_sc[qi]
        n_sub = z.shape[0] // kb
        ws = [None] * n_sub
        for s in reversed(range(n_sub)):
            rows = slice(s * kb, (s + 1) * kb)
            w = jnp.exp2(z[rows] - (incl[rows] + carry))
            if mask is not None:
                w = jnp.where(mask[rows], w, 0.0)
            ws[s] = w.astype(BF16)
            carry = carry + incl[s * kb:s * kb + 1, :]
        carry_sc[qi] = carry
        w_all = ws[0] if n_sub == 1 else jnp.concatenate(ws, axis=0)
        acc_sc[qi] += jnp.dot(vt_grp, w_all, preferred_element_type=F32)

    def run_unrolled(tiles):
        n = len(tiles)
        z, hilo, incl = {}, {}, {}
        for tau in range(n + 2):
            if tau < n:
                k_fn, _, qi, mask = tiles[tau]
                z[tau] = stage_a(k_fn(), qt_sc[qi])
            if 1 <= tau <= n:
                incl[tau - 1] = stage_c(hilo.pop(tau - 1))
            if tau >= 2:
                _, vt_fn, qi2, mask2 = tiles[tau - 2]
                stage_de(z.pop(tau - 2), incl.pop(tau - 2), mask2, vt_fn(), qi2)
            if tau < n:
                hilo[tau] = stage_b(z[tau], mask)

    def run_loop(n, base, k_of, vt_of):
        def finish(tau, slot):
            stage_de(z_sc[slot], incl_sc[slot], None, vt_of(gtab_ref[base + tau]), itab_ref[base + tau])

        zt = stage_a(k_of(gtab_ref[base]), qt_sc[itab_ref[base]])
        z_sc[0] = zt
        hilo_sc[0] = stage_b(zt, None)
        zt = stage_a(k_of(gtab_ref[base + 1]), qt_sc[itab_ref[base + 1]])
        incl_sc[0] = stage_c(hilo_sc[0])
        z_sc[1] = zt
        hilo_sc[1] = stage_b(zt, None)

        def step(tau, c):
            slot = tau % 2
            zt = stage_a(k_of(gtab_ref[base + tau]), qt_sc[itab_ref[base + tau]])
            inc = stage_c(hilo_sc[1 - slot])
            finish(tau - 2, slot)
            z_sc[slot] = zt
            hilo_sc[slot] = stage_b(zt, None)
            incl_sc[1 - slot] = inc
            return c

        lax.fori_loop(2, n, step, 0)
        slot = n % 2
        inc = stage_c(hilo_sc[1 - slot])
        finish(n - 2, slot)
        incl_sc[1 - slot] = inc
        finish(n - 1, 1 - slot)

    diag_mask = (lax.broadcasted_iota(jnp.int32, (qb, qb), 0)
                 < lax.broadcasted_iota(jnp.int32, (qb, qb), 1))
    run_unrolled([((lambda g=g: kn_sc[g * qb:(g + 1) * qb, :]), (lambda g=g: vnt_sc[g]), g, diag_mask)
                  for g in range(n_q)])
    n_earlier = n_q * (n_q - 1) // 2
    new_k = lambda g: kn_sc[pl.ds(pl.multiple_of(g * qb, qb), qb), :]
    if n_earlier == 1:
        run_unrolled([((lambda: kn_sc[0:qb, :]), (lambda: vnt_sc[0]), 1, None)])
    elif n_earlier:
        run_loop(n_earlier, 0, new_k, lambda g: vnt_sc[g])

    rem_tiles = []
    for j in reversed(range(n_prem_blocks)):
        valid = min(kb, n_prem - j * kb)
        mask = None if valid == kb else lax.broadcasted_iota(jnp.int32, (kb, qb), 0) < valid
        rem_tiles += [((lambda j=j: kr_sc[j]), (lambda j=j: vrt_sc[j]), qi, mask) for qi in range(n_q)]
    if rem_tiles:
        run_unrolled(rem_tiles)
    n_ptiles = n_q * n_pfull
    past_k = lambda g: kp_sc[pl.ds(pl.multiple_of(g * pg, pg), pg), :]
    if n_ptiles == 1:
        run_unrolled([((lambda: kp_sc[0:pg, :]), (lambda: vpt_sc[0]), 0, None)])
    elif n_ptiles:
        run_loop(n_ptiles, n_earlier, past_k, lambda g: vpt_sc[g])

    for g in range(n_q):
        o_ref[0, g * qb:(g + 1) * qb, :] = acc_sc[g].T.astype(o_ref.dtype)


def _stick_breaking(q_arr, q_col, k_arr, v_arr, past_k, past_v, past_layer, n_past_valid):
    bsz, tq, _ = q_arr.shape
    p = past_k.shape[2]
    qb = 2 * SB_BLOCK if tq % (2 * SB_BLOCK) == 0 else SB_BLOCK
    n_q = tq // qb
    pg = 2 * SB_BLOCK
    n_pfull = n_past_valid // pg
    n_prem_blocks = -(-(n_past_valid - n_pfull * pg) // SB_BLOCK)
    assert k_arr.shape[1] == tq and p % SB_BLOCK == 0 and n_past_valid <= p
    tiles = [(i, g) for i in range(n_q) for g in reversed(range(i))]
    tiles += [(i, g) for i in range(n_q) for g in reversed(range(n_pfull))]
    tiles = tiles or [(0, 0)]
    itab = jnp.asarray([t[0] for t in tiles], jnp.int32)
    gtab = jnp.asarray([t[1] for t in tiles], jnp.int32)
    past_map = ((lambda b, h, it, gt: (past_layer, b, 0, h)) if past_k.shape[1] == bsz
                else (lambda b, h, it, gt: (past_layer, 0, 0, h)))
    grid_spec = pltpu.PrefetchScalarGridSpec(
        num_scalar_prefetch=2,
        grid=(bsz, H_SB),
        in_specs=[
            pl.BlockSpec((1, tq, D_SB), lambda b, h, it, gt: (b, 0, q_col // D_SB + h)),
            pl.BlockSpec((1, tq, D_SB), lambda b, h, it, gt: (b, 0, h)),
            pl.BlockSpec((1, tq, D_SB), lambda b, h, it, gt: (b, 0, h)),
            pl.BlockSpec((None, 1, p, D_SB), past_map),
            pl.BlockSpec((None, 1, p, D_SB), past_map),
        ],
        out_specs=pl.BlockSpec((1, tq, D_SB), lambda b, h, it, gt: (b, 0, h)),
        scratch_shapes=[
            pltpu.VMEM((n_q, D_SB, qb), BF16),
            pltpu.VMEM((tq, D_SB), BF16),
            pltpu.VMEM((n_q, D_SB, qb), BF16),
            pltpu.VMEM((max(n_pfull, 1) * pg, D_SB), BF16),
            pltpu.VMEM((max(n_pfull, 1), D_SB, pg), BF16),
            pltpu.VMEM((max(n_prem_blocks, 1), SB_BLOCK, D_SB), BF16),
            pltpu.VMEM((max(n_prem_blocks, 1), D_SB, SB_BLOCK), BF16),
            pltpu.VMEM((n_q, D_SB, qb), F32),
            pltpu.VMEM((n_q, 1, qb), F32),
            pltpu.VMEM((2, max(qb, pg), qb), F32),
            pltpu.VMEM((2, 2 * max(qb, pg), qb), BF16),
            pltpu.VMEM((2, max(qb, pg), qb), F32),
        ])
    return pl.pallas_call(
        functools.partial(_stick_breaking_body, qb=qb, n_q=n_q, n_past_valid=n_past_valid),
        grid_spec=grid_spec,
        out_shape=jax.ShapeDtypeStruct((bsz, tq, H_SB * D_SB), BF16),
        compiler_params=_params("parallel", "parallel"),
        name="stick_breaking",
    )(itab, gtab, q_arr, k_arr, v_arr, past_k, past_v)


def _pool_body(u_ref, prev_ref, buf_ref, w_ref, sc_ref, o_ref, z_sc, *, tm, n_hist):
    i = pl.program_id(1)
    z_sc[POOL_HALO:POOL_HALO + tm, :] = u_ref[0]

    @pl.when(i == 0)
    def _():
        z_sc[0:POOL_HALO, :] = buf_ref[0]

    @pl.when(i > 0)
    def _():
        z_sc[0:POOL_HALO, :] = prev_ref[0]

    tr = min(tm, 256)
    for g, win in enumerate(POOL_WINDOWS):
        cols = slice(g * DG_POOL, (g + 1) * DG_POOL)
        for r0 in range(0, tm, tr):
            base = POOL_HALO + r0
            wsum = z_sc[base:base + tr, cols]
            for back in range(1, win):
                wsum = wsum + z_sc[base - back:base - back + tr, cols]
            t_idx = i * tm + r0 + lax.broadcasted_iota(jnp.int32, (tr, 1), 0)
            cnt = jnp.minimum(t_idx + (n_hist + 1), win).astype(F32)
            pooled = wsum / cnt - z_sc[base:base + tr, cols]
            mixed = jnp.dot(pooled.astype(BF16), w_ref[g], preferred_element_type=F32)
            o_ref[0, r0:r0 + tr, cols] = (mixed * sc_ref[:, cols]).astype(o_ref.dtype)


def _pool_mixer(proj3, hist, n_hist, mix_w, scale):
    bsz, t, _ = proj3.shape
    tm = min(t, 512)
    hpb = tm // POOL_HALO
    ucol = COL_U // D_MODEL
    hist_map = (lambda b, i: (b, 0, 0)) if hist.shape[0] == bsz else (lambda b, i: (0, 0, 0))
    return pl.pallas_call(
        functools.partial(_pool_body, tm=tm, n_hist=n_hist),
        grid=(bsz, t // tm),
        in_specs=[
            pl.BlockSpec((1, tm, D_MODEL), lambda b, i: (b, i, ucol)),
            pl.BlockSpec((1, POOL_HALO, D_MODEL),
                         lambda b, i: (b, jnp.maximum(i * hpb - 1, 0), ucol)),
            pl.BlockSpec((1, POOL_HALO, D_MODEL), hist_map),
            pl.BlockSpec((len(POOL_WINDOWS), DG_POOL, DG_POOL), lambda b, i: (0, 0, 0)),
            pl.BlockSpec((1, D_MODEL), lambda b, i: (0, 0)),
        ],
        out_specs=pl.BlockSpec((1, tm, D_MODEL), lambda b, i: (b, i, 0)),
        out_shape=jax.ShapeDtypeStruct((bsz, t, D_MODEL), BF16),
        scratch_shapes=[pltpu.VMEM((POOL_HALO + tm, D_MODEL), F32)],
        compiler_params=_params("parallel", "arbitrary"),
        name="pool_mixer",
    )(proj3, proj3, hist, mix_w, scale.reshape(1, D_MODEL))


def _merge_body(or_ref, os_ref, op_ref, g0_ref, g1_ref, g2_ref, h_ref, wb_ref, wo_ref, g_ref, b_ref,
                o_ref):
    mix = jax.nn.sigmoid(g0_ref[...]) * jnp.dot(or_ref[...], wb_ref[0], preferred_element_type=F32)
    mix = mix + jax.nn.sigmoid(g1_ref[...]) * jnp.dot(os_ref[...], wb_ref[1],
                                                      preferred_element_type=F32)
    mix = mix + jax.nn.sigmoid(g2_ref[...]) * jnp.dot(op_ref[...], wb_ref[2],
                                                      preferred_element_type=F32)
    out = jnp.dot(mix.astype(BF16), wo_ref[...], preferred_element_type=F32)
    o_ref[...] = _layer_norm(ALPHA * h_ref[...] + out, g_ref[...], b_ref[...])


def _merge(o_r, o_s, o_p, proj, h, w_branch, w_out, g, b):
    m = h.shape[0]
    tm = min(m, 512)
    row = lambda i: (i, 0)
    gcol = COL_GATES // D_MODEL
    return pl.pallas_call(
        _merge_body,
        grid=(m // tm,),
        in_specs=[
            pl.BlockSpec((tm, D_MODEL), row),
            pl.BlockSpec((tm, D_MODEL), row),
            pl.BlockSpec((tm, D_MODEL), row),
            pl.BlockSpec((tm, D_MODEL), lambda i: (i, gcol)),
            pl.BlockSpec((tm, D_MODEL), lambda i: (i, gcol + 1)),
            pl.BlockSpec((tm, D_MODEL), lambda i: (i, gcol + 2)),
            pl.BlockSpec((tm, D_MODEL), row),
            pl.BlockSpec((3, D_MODEL, D_MODEL), lambda i: (0, 0, 0)),
            pl.BlockSpec((D_MODEL, D_MODEL), lambda i: (0, 0)),
            pl.BlockSpec((1, D_MODEL), lambda i: (0, 0)),
            pl.BlockSpec((1, D_MODEL), lambda i: (0, 0)),
        ],
        out_specs=pl.BlockSpec((tm, D_MODEL), row),
        out_shape=jax.ShapeDtypeStruct((m, D_MODEL), F32),
        compiler_params=_params("parallel"),
        name="merge",
    )(o_r, o_s, o_p, proj, proj, proj, h, w_branch, w_out, g.reshape(1, D_MODEL),
      b.reshape(1, D_MODEL))


def _pad_rows(a, rows):
    return jnp.pad(a, ((0, 0), (0, rows - a.shape[1]), (0, 0)))


def _trunk_layer(h, past_k, past_v, past_layer, n_past, prefix_kv, s0, pool_buf, pos0, lw):
    bsz, t, _ = h.shape
    m = bsz * t
    h1, h1b = _ffn_ln(h.reshape(m, D_MODEL), lw["up1"], lw["down1"], lw["ln_g"][0], lw["ln_b"][0],
                      emit_bf16=True)
    proj = _in_proj(h1b, lw["w_in"])
    proj3 = proj.reshape(bsz, t, D_PROJ)
    n_prefix = 0 if prefix_kv is None else prefix_kv[0].shape[1]
    k_s, v_s, kb, vb = _kv_proj(h1b, lw["w_in"], bsz, t, n_prefix)
    if prefix_kv is not None:
        front = lambda a: jnp.broadcast_to(a, (bsz,) + a.shape[1:])
        k_s = lax.dynamic_update_slice(k_s, front(prefix_kv[0]), (0, 0, 0))
        v_s = lax.dynamic_update_slice(v_s, front(prefix_kv[1]), (0, 0, 0))

    o_r, s_new = _retention(proj3, pos0, s0, lw["ret_g"])

    if t % SB_BLOCK == 0:
        o_s = _stick_breaking(proj3, COL_QS, kb, vb, past_k, past_v, past_layer, n_past)
    else:
        q = _pad_rows(proj3[:, :, COL_QS:COL_QS + D_MODEL], SB_BLOCK)
        o_s = _stick_breaking(q, 0, _pad_rows(kb, SB_BLOCK), _pad_rows(vb, SB_BLOCK), past_k, past_v,
                              past_layer, n_past)[:, :t]

    n_hist = pool_buf.shape[1]
    hist = jnp.pad(pool_buf, ((0, 0), (POOL_HALO - n_hist, 0), (0, 0)))
    o_p = _pool_mixer(proj3, hist, n_hist, lw["pool_w"], lw["pool_scale"])

    h2 = _merge(o_r.reshape(m, D_MODEL), o_s.reshape(m, D_MODEL), o_p.reshape(m, D_MODEL), proj, h1,
                lw["w_branch"], lw["w_out"], lw["ln_g"][1], lw["ln_b"][1])
    h3 = _ffn_ln(h2, lw["up2"], lw["down2"], lw["ln_g"][2], lw["ln_b"][2], emit_bf16=False)

    u = proj3[:, :, COL_U:COL_U + D_MODEL]
    assert t >= POOL_BUF
    buf_new = u[:, t - POOL_BUF:]
    return h3.reshape(bsz, t, D_MODEL), (k_s, v_s, s_new, buf_new)


def _layer_weights(l, w_in, ret_norm_g, pool_mix_w, pool_scale, w_branch, w_out,
                   ffn1_up, ffn1_down, ffn2_up, ffn2_down, ln_g, ln_b):
    return dict(
        w_in=w_in[l].astype(BF16), ret_g=ret_norm_g[l], pool_w=pool_mix_w[l].astype(BF16),
        pool_scale=pool_scale[l], w_branch=w_branch[l].astype(BF16), w_out=w_out[l].astype(BF16),
        up1=ffn1_up[l].astype(BF16), down1=ffn1_down[l].astype(BF16),
        up2=ffn2_up[l].astype(BF16), down2=ffn2_down[l].astype(BF16),
        ln_g=ln_g[l], ln_b=ln_b[l])


def kernel(x_prompt, x_sample, cache_sb_k, cache_sb_v, state_ret, state_pool, meta_tokens, w_in, ret_norm_g, pool_mix_w, pool_scale, w_branch, w_out, ffn1_up, ffn1_down, ffn2_up, ffn2_down, ln_g, ln_b):
    bsz = x_prompt.shape[0]
    dec_b, past_len = cache_sb_k.shape[1], cache_sb_k.shape[2]
    assert past_len % SB_BLOCK == 0
    h_meta = meta_tokens[None]
    h_p = x_prompt
    h_s = x_sample
    cache_k = cache_sb_k.reshape(DEPTH, dec_b, past_len, D_MODEL)
    cache_v = cache_sb_v.reshape(DEPTH, dec_b, past_len, D_MODEL)
    no_past = jnp.zeros((1, 1, SB_BLOCK, D_MODEL), F32)
    outs = [[] for _ in range(8)]
    for l in range(DEPTH):
        lw = _layer_weights(l, w_in, ret_norm_g, pool_mix_w, pool_scale, w_branch, w_out,
                            ffn1_up, ffn1_down, ffn2_up, ffn2_down, ln_g, ln_b)
        h_meta, (mk, mv, ms, mbuf) = _trunk_layer(
            h_meta, no_past, no_past, 0, 0, None, jnp.zeros((1, H_RET, DK_RET, DV_RET), F32),
            jnp.zeros((1, 0, D_MODEL), F32), 0, lw)
        h_p, (pk, pv, ps, pbuf) = _trunk_layer(
            h_p, _pad_rows(mk, SB_BLOCK)[None], _pad_rows(mv, SB_BLOCK)[None], 0, N_META, (mk, mv),
            ms, mbuf, N_META, lw)
        h_s, (sk, sv, ss, sbuf) = _trunk_layer(
            h_s, cache_k, cache_v, l, past_len, None, state_ret[l], state_pool[l],
            N_META + past_len, lw)
        heads = lambda a: a.reshape(a.shape[0], a.shape[1], H_SB, D_SB)
        outs[0].append(heads(pk))
        outs[1].append(heads(pv))
        outs[2].append(ps)
        outs[3].append(pbuf)
        outs[4].append(heads(sk))
        outs[5].append(heads(sv))
        outs[6].append(ss)
        outs[7].append(sbuf)
    return (h_p, h_s) + tuple(jnp.stack(o) for o in outs)
```

```python
import functools
import math

import jax
import jax.numpy as jnp
from jax import lax
from jax.experimental import pallas as pl
from jax.experimental.pallas import tpu as pltpu

D_MODEL = 1024
DEPTH = 2
N_META = 16
H_RET = 4
DK_RET = D_MODEL // 8
DV_RET = 2 * DK_RET
H_SB = 8
D_SB = D_MODEL // 8
RET_CHUNK = 256
POOL_WINDOWS = (2, 4, 8, 16)
DG_POOL = D_MODEL // len(POOL_WINDOWS)
POOL_BUF = max(POOL_WINDOWS) - 1
POOL_HALO = 16
D_FF = ((8 * D_MODEL // 3 + 127) // 128) * 128
D_IN = 10 * D_MODEL
ROPE_BASE = 10000.0
LN_EPS = 1e-5
LOG2E = 1.0 / math.log(2.0)
ALPHA = (2 * DEPTH) ** 0.25

W_COL_KS, W_COL_VS = 4096, 5120
D_PROJ = D_IN - 2 * D_MODEL
COL_QR, COL_KR, COL_VR, COL_GR = 0, 512, 1024, 2048
COL_QS, COL_U, COL_GATES = 3072, 4096, 5120

SB_BLOCK = 128
SB_UNROLL = 2
BF16_ROWS = 16
VMEM_LIMIT = 48 * 1024 * 1024
FFN_VMEM_LIMIT = 56 * 1024 * 1024

F32 = jnp.float32
BF16 = jnp.bfloat16


def _params(*sem):
    return pltpu.CompilerParams(dimension_semantics=sem, vmem_limit_bytes=VMEM_LIMIT)


def _layer_norm(y, g, b):
    mu = jnp.mean(y, axis=-1, keepdims=True)
    d = y - mu
    var = jnp.mean(d * d, axis=-1, keepdims=True)
    return d * lax.rsqrt(var + LN_EPS) * g + b


def _silu(x):
    return x * jax.nn.sigmoid(x)


def _ffn_ln_body(x_ref, wg_ref, wu_ref, wd_ref, g_ref, b_ref, o_ref, *maybe_ob_ref):
    x = x_ref[...]
    xb = x.astype(BF16)
    gate = jnp.dot(xb, wg_ref[...], preferred_element_type=F32)
    up = jnp.dot(xb, wu_ref[...], preferred_element_type=F32)
    act = (_silu(gate) * up).astype(BF16)
    ffn = jnp.dot(act, wd_ref[...], preferred_element_type=F32)
    h = _layer_norm(ALPHA * x + 0.5 * ffn, g_ref[...], b_ref[...])
    o_ref[...] = h
    for ob_ref in maybe_ob_ref:
        ob_ref[...] = h.astype(BF16)


def _ffn_ln(x, w_up, w_down, g, b, emit_bf16):
    m = x.shape[0]
    tm = min(m, 512)
    assert m % tm == 0
    row = lambda i: (i, 0)
    once = pl.Buffered(1)
    out_shape = [jax.ShapeDtypeStruct((m, D_MODEL), F32)]
    out_specs = [pl.BlockSpec((tm, D_MODEL), row)]
    if emit_bf16:
        out_shape.append(jax.ShapeDtypeStruct((m, D_MODEL), BF16))
        out_specs.append(pl.BlockSpec((tm, D_MODEL), row))
    res = pl.pallas_call(
        _ffn_ln_body,
        grid=(m // tm,),
        in_specs=[
            pl.BlockSpec((tm, D_MODEL), row),
            pl.BlockSpec((D_MODEL, D_FF), lambda i: (0, 0), pipeline_mode=once),
            pl.BlockSpec((D_MODEL, D_FF), lambda i: (0, 1), pipeline_mode=once),
            pl.BlockSpec((D_FF, D_MODEL), lambda i: (0, 0), pipeline_mode=once),
            pl.BlockSpec((1, D_MODEL), lambda i: (0, 0)),
            pl.BlockSpec((1, D_MODEL), lambda i: (0, 0)),
        ],
        out_specs=out_specs,
        out_shape=out_shape,
        compiler_params=pltpu.CompilerParams(dimension_semantics=("parallel",),
                                             vmem_limit_bytes=FFN_VMEM_LIMIT),
        name="ffn_ln",
    )(x, w_up, w_up, w_down, g.reshape(1, D_MODEL), b.reshape(1, D_MODEL))
    return res if emit_bf16 else res[0]


def _in_proj_body(x_ref, w_ref, *rest):
    proj_ref, kf_ref, vf_ref, kb_ref, vb_ref = rest[-5:]
    x = x_ref[...]
    for c0 in range(0, D_PROJ, D_MODEL):
        w0 = c0 if c0 < W_COL_KS else c0 + 2 * D_MODEL
        proj_ref[:, c0:c0 + D_MODEL] = jnp.dot(x, w_ref[:, w0:w0 + D_MODEL],
                                               preferred_element_type=F32)
    k = jnp.dot(x, w_ref[:, W_COL_KS:W_COL_KS + D_MODEL], preferred_element_type=F32)
    v = jnp.dot(x, w_ref[:, W_COL_VS:W_COL_VS + D_MODEL], preferred_element_type=F32)
    kb_ref[0] = k.astype(BF16)
    vb_ref[0] = v.astype(BF16)
    kf_ref[0, 0] = k.reshape(k.shape[0], H_SB, D_SB)
    vf_ref[0, 0] = v.reshape(v.shape[0], H_SB, D_SB)


def _in_proj(xb, w_in, bsz, t, row0, layer, kv_all):
    tm = min(t, 256)
    assert t % tm == 0
    nt = t // tm
    half = pl.BlockSpec((1, tm, D_MODEL), lambda b, i: (b, i, 0))
    if row0:
        full = pl.BlockSpec((pl.Element(1), pl.Element(1), pl.Element(tm), pl.Element(H_SB),
                             pl.Element(D_SB)), lambda b, i: (layer, b, row0 + i * tm, 0, 0))
    else:
        full = pl.BlockSpec((1, 1, tm, H_SB, D_SB), lambda b, i: (layer, b, i, 0, 0))
    in_specs = [
        pl.BlockSpec((tm, D_MODEL), lambda b, i: (b * nt + i, 0)),
        pl.BlockSpec((D_MODEL, D_IN), lambda b, i: (0, 0), pipeline_mode=pl.Buffered(1)),
    ]
    operands = [xb, w_in]
    aliases = {}
    if kv_all is not None:
        in_specs += [pl.BlockSpec(memory_space=pl.ANY)] * 2
        operands += list(kv_all)
        aliases = {2: 1, 3: 2}
    res = pl.pallas_call(
        _in_proj_body,
        grid=(bsz, nt),
        in_specs=in_specs,
        out_specs=[pl.BlockSpec((tm, D_PROJ), lambda b, i: (b * nt + i, 0)), full, full, half, half],
        out_shape=[jax.ShapeDtypeStruct((bsz * t, D_PROJ), F32)]
        + [jax.ShapeDtypeStruct((DEPTH, bsz, row0 + t, H_SB, D_SB), F32)] * 2
        + [jax.ShapeDtypeStruct((bsz, t, D_MODEL), BF16)] * 2,
        input_output_aliases=aliases,
        compiler_params=_params("parallel", "arbitrary"),
        name="in_proj",
    )(*operands)
    return res[0], (res[1], res[2]), res[3], res[4]


def _retention_body(q_ref, k_ref, v_ref, gr_ref, cos_ref, sin_ref, s0_ref, dintra_ref, dq_ref,
                    dk_ref, dc_ref, rg_ref, o_ref, snew_ref, qr_sc, kr_sc, kd_sc, vb_sc, s_sc,
                    *, chunk, n_chunks):
    d_intra = dintra_ref[0]
    d_q = dq_ref[0]
    d_k = dk_ref[0]
    d_c = dc_ref[0]
    rg = rg_ref[0]

    def rot(x, cos, sin):
        return x * cos + pltpu.roll(x, DK_RET // 2, 1) * sin

    for c in range(n_chunks):
        rows = slice(c * chunk, (c + 1) * chunk)
        cos = cos_ref[rows, :]
        sin = sin_ref[rows, :]
        k = rot(k_ref[0, rows, :], cos, sin) * (DK_RET ** -0.5)
        qr_sc[rows, :] = rot(q_ref[0, rows, :], cos, sin).astype(BF16)
        kr_sc[rows, :] = k.astype(BF16)
        kd_sc[rows, :] = (k * d_k).astype(BF16)
        vb_sc[rows, :] = v_ref[0, rows, :].astype(BF16)

    s = s0_ref[0, 0]
    for c in range(n_chunks):
        rows = slice(c * chunk, (c + 1) * chunk)
        s_sc[c] = s.astype(BF16)
        s = s * d_c + lax.dot_general(kd_sc[rows, :], vb_sc[rows, :], (((0,), (0,)), ((), ())),
                                      preferred_element_type=F32)
    snew_ref[0, 0] = s

    def products(c):
        rows = slice(c * chunk, (c + 1) * chunk)
        qc = qr_sc[rows, :]
        att = lax.dot_general(qc, kr_sc[rows, :], (((1,), (1,)), ((), ())),
                              preferred_element_type=F32)
        return att, jnp.dot(qc, s_sc[c], preferred_element_type=F32)

    nxt = products(0)
    for c in range(n_chunks):
        rows = slice(c * chunk, (c + 1) * chunk)
        att, inter = nxt
        if c + 1 < n_chunks:
            nxt = products(c + 1)
        o = (jnp.dot((att * d_intra).astype(BF16), vb_sc[rows, :], preferred_element_type=F32)
             + inter * d_q)
        mu = jnp.mean(o, axis=-1, keepdims=True)
        d = o - mu
        var = jnp.mean(d * d, axis=-1, keepdims=True)
        on = d * lax.rsqrt(var + LN_EPS) * rg
        o_ref[0, rows, :] = (on * _silu(gr_ref[0, rows, :])).astype(o_ref.dtype)


def _retention_decays(chunk):
    lg = jnp.log1p(-jnp.exp(jnp.linspace(math.log(1.0 / 32), math.log(1.0 / 512), H_RET, dtype=F32)))
    i = jnp.arange(chunk, dtype=F32)
    diff = i[:, None] - i[None, :]
    d_intra = jnp.where(diff >= 0, jnp.exp(lg[:, None, None] * jnp.maximum(diff, 0.0)), 0.0)
    d_q = jnp.exp(lg[:, None] * (i[None, :] + 1.0))
    d_k = jnp.exp(lg[:, None] * (chunk - 1.0 - i[None, :]))
    d_c = jnp.exp(lg * chunk)
    return (d_intra,
            jnp.broadcast_to(d_q[:, :, None], (H_RET, chunk, DV_RET)),
            jnp.broadcast_to(d_k[:, :, None], (H_RET, chunk, DK_RET)),
            jnp.broadcast_to(d_c[:, None, None], (H_RET, 1, DV_RET)))


def _rotary_tables(pos0, t):
    half = DK_RET // 2
    inv_freq = ROPE_BASE ** (-jnp.arange(half, dtype=F32) / half)
    ang = (pos0 + jnp.arange(t, dtype=F32))[:, None] * inv_freq[None, :]
    cos = jnp.cos(ang)
    sin = jnp.sin(ang)
    return jnp.concatenate([cos, cos], axis=-1), jnp.concatenate([-sin, sin], axis=-1)


def _retention(proj3, pos0, s0, ret_g):
    bsz, t, _ = proj3.shape
    chunk = min(RET_CHUNK, t)
    n_chunks = t // chunk
    cos, sin = _rotary_tables(pos0, t)
    d_intra, d_q, d_k, d_c = _retention_decays(chunk)
    s0_map = (lambda b, h: (b, h, 0, 0)) if s0.shape[0] == bsz else (lambda b, h: (0, h, 0, 0))
    per_head = lambda b, h: (h, 0, 0)
    return pl.pallas_call(
        functools.partial(_retention_body, chunk=chunk, n_chunks=n_chunks),
        grid=(bsz, H_RET),
        in_specs=[
            pl.BlockSpec((1, t, DK_RET), lambda b, h: (b, 0, COL_QR // DK_RET + h)),
            pl.BlockSpec((1, t, DK_RET), lambda b, h: (b, 0, COL_KR // DK_RET + h)),
            pl.BlockSpec((1, t, DV_RET), lambda b, h: (b, 0, COL_VR // DV_RET + h)),
            pl.BlockSpec((1, t, DV_RET), lambda b, h: (b, 0, COL_GR // DV_RET + h)),
            pl.BlockSpec((t, DK_RET), lambda b, h: (0, 0)),
            pl.BlockSpec((t, DK_RET), lambda b, h: (0, 0)),
            pl.BlockSpec((1, 1, DK_RET, DV_RET), s0_map),
            pl.BlockSpec((1, chunk, chunk), per_head),
            pl.BlockSpec((1, chunk, DV_RET), per_head),
            pl.BlockSpec((1, chunk, DK_RET), per_head),
            pl.BlockSpec((1, 1, DV_RET), per_head),
            pl.BlockSpec((1, 1, DV_RET), per_head),
        ],
        out_specs=[
            pl.BlockSpec((1, t, DV_RET), lambda b, h: (b, 0, h)),
            pl.BlockSpec((1, 1, DK_RET, DV_RET), lambda b, h: (b, h, 0, 0)),
        ],
        out_shape=[
            jax.ShapeDtypeStruct((bsz, t, H_RET * DV_RET), BF16),
            jax.ShapeDtypeStruct((bsz, H_RET, DK_RET, DV_RET), F32),
        ],
        scratch_shapes=[pltpu.VMEM((t, DK_RET), BF16), pltpu.VMEM((t, DK_RET), BF16),
                        pltpu.VMEM((t, DK_RET), BF16), pltpu.VMEM((t, DV_RET), BF16),
                        pltpu.VMEM((n_chunks, DK_RET, DV_RET), BF16)],
        compiler_params=_params("parallel", "parallel"),
        name="retention",
    )(proj3, proj3, proj3, proj3, cos, sin, s0, d_intra, d_q, d_k, d_c,
      ret_g.reshape(H_RET, 1, DV_RET))


def _stick_breaking_body(itab_ref, gtab_ref, q_ref, kn_ref, vn_ref, kp_ref, vp_ref, o_ref,
                         qt_sc, kn_sc, vnt_sc, kp_sc, vpt_sc, kr_sc, vrt_sc, acc_sc, carry_sc,
                         z_sc, hilo_sc, incl_sc, *, qb, n_q, n_past_valid):
    kb = SB_BLOCK
    pg = 2 * kb
    n_pfull = n_past_valid // pg
    n_prem = n_past_valid - n_pfull * pg
    n_prem_blocks = -(-n_prem // kb)
    scale = D_SB ** -0.5

    kn_sc[...] = kn_ref[0].astype(BF16)
    for g in range(n_q):
        rows = slice(g * qb, (g + 1) * qb)
        vnt_sc[g] = vn_ref[0, rows, :].T.astype(BF16)
        qt_sc[g] = q_ref[0, rows, :].T.astype(BF16)
    if n_pfull:
        kp_sc[...] = kp_ref[0, 0:n_pfull * pg, :].astype(BF16)
        for g in range(n_pfull):
            vpt_sc[g] = vp_ref[0, g * pg:(g + 1) * pg, :].T.astype(BF16)
    for j in range(n_prem_blocks):
        rows = slice(n_pfull * pg + j * kb, n_pfull * pg + (j + 1) * kb)
        kr_sc[j] = kp_ref[0, rows, :].astype(BF16)
        vrt_sc[j] = vp_ref[0, rows, :].T.astype(BF16)
    acc_sc[...] = jnp.zeros_like(acc_sc)
    carry_sc[...] = jnp.zeros_like(carry_sc)

    def suffix_ones2(sb):
        ones = (lax.broadcasted_iota(jnp.int32, (sb, sb), 1)
                >= lax.broadcasted_iota(jnp.int32, (sb, sb), 0)).astype(BF16)
        return jnp.concatenate([ones, ones], axis=1)

    last_valid = n_prem - (n_prem_blocks - 1) * kb if n_prem_blocks else kb
    last_rows = -(-last_valid // BF16_ROWS) * BF16_ROWS
    suffix_mats = {sb: suffix_ones2(sb) for sb in {kb, last_rows}}

    def stage_a(k_grp, q_t):
        return jnp.dot(k_grp, q_t, preferred_element_type=F32) * (scale * LOG2E)

    def stage_b(z, mask):
        neg_abs = lax.bitcast_convert_type(
            lax.bitcast_convert_type(z, jnp.uint32) | jnp.uint32(0x80000000), F32)
        go = jnp.maximum(z, 0.0) + jnp.log(1.0 + jnp.exp2(neg_abs)) * LOG2E
        if mask is not None:
            go = jnp.where(mask, go, 0.0)
        hi = go.astype(BF16)
        lo = (go - hi.astype(F32)).astype(BF16)
        sb = min(kb, z.shape[0])
        parts = []
        for s in range(z.shape[0] // sb):
            parts += [hi[s * sb:(s + 1) * sb], lo[s * sb:(s + 1) * sb]]
        return jnp.concatenate(parts, axis=0)

    def stage_c(hilo):
        sb = min(kb, hilo.shape[0] // 2)
        parts = [jnp.dot(suffix_mats[sb], hilo[s * 2 * sb:(s + 1) * 2 * sb],
                         preferred_element_type=F32) for s in range(hilo.shape[0] // (2 * sb))]
        return parts[0] if len(parts) == 1 else jnp.concatenate(parts, axis=0)

    def stage_de(z, incl, mask, vt_grp, qi):
        carry = carry_sc[qi]
        sb = min(kb, z.shape[0])
        n_sub = z.shape[0] // sb
        ws = [None] * n_sub
        for s in reversed(range(n_sub)):
            rows = slice(s * sb, (s + 1) * sb)
            w = jnp.exp2(z[rows] - (incl[rows] + carry))
            if mask is not None:
                w = jnp.where(mask[rows], w, 0.0)
            ws[s] = w.astype(BF16)
            carry = carry + incl[s * sb:s * sb + 1, :]
        carry_sc[qi] = carry
        w_all = ws[0] if n_sub == 1 else jnp.concatenate(ws, axis=0)
        acc_sc[qi] += jnp.dot(vt_grp, w_all, preferred_element_type=F32)

    def run_unrolled(tiles):
        n = len(tiles)
        z, hilo, incl = {}, {}, {}
        for tau in range(n + 2):
            if tau < n:
                k_fn, _, qi, mask = tiles[tau]
                z[tau] = stage_a(k_fn(), qt_sc[qi])
            if 1 <= tau <= n:
                incl[tau - 1] = stage_c(hilo.pop(tau - 1))
            if tau >= 2:
                _, vt_fn, qi2, mask2 = tiles[tau - 2]
                stage_de(z.pop(tau - 2), incl.pop(tau - 2), mask2, vt_fn(), qi2)
            if tau < n:
                hilo[tau] = stage_b(z[tau], mask)

    def run_loop(n, base, k_of, vt_of):
        u = SB_UNROLL if n % SB_UNROLL == 0 and n >= 2 * SB_UNROLL else 1
        n_steps = n // u
        tile = lambda step, k: base + step * u + k

        def a_all(step):
            return [stage_a(k_of(gtab_ref[tile(step, k)]), qt_sc[itab_ref[tile(step, k)]])
                    for k in range(u)]

        def c_all(slot):
            return [stage_c(hilo_sc[slot * u + k]) for k in range(u)]

        def b_store(zs, slot):
            for k in range(u):
                z_sc[slot * u + k] = zs[k]
                hilo_sc[slot * u + k] = stage_b(zs[k], None)

        def c_store(incs, slot):
            for k in range(u):
                incl_sc[slot * u + k] = incs[k]

        def finish(step, slot):
            for k in range(u):
                stage_de(z_sc[slot * u + k], incl_sc[slot * u + k], None,
                         vt_of(gtab_ref[tile(step, k)]), itab_ref[tile(step, k)])

        b_store(a_all(0), 0)
        zs = a_all(1)
        incs = c_all(0)
        b_store(zs, 1)
        c_store(incs, 0)

        def body(step, c):
            slot = step % 2
            zs = a_all(step)
            incs = c_all(1 - slot)
            finish(step - 2, slot)
            b_store(zs, slot)
            c_store(incs, 1 - slot)
            return c

        lax.fori_loop(2, n_steps, body, 0)
        slot = n_steps % 2
        incs = c_all(1 - slot)
        finish(n_steps - 2, slot)
        c_store(incs, 1 - slot)
        finish(n_steps - 1, 1 - slot)

    diag_mask = (lax.broadcasted_iota(jnp.int32, (qb, qb), 0)
                 < lax.broadcasted_iota(jnp.int32, (qb, qb), 1))
    run_unrolled([((lambda g=g: kn_sc[g * qb:(g + 1) * qb, :]), (lambda g=g: vnt_sc[g]), g, diag_mask)
                  for g in range(n_q)])
    n_earlier = n_q * (n_q - 1) // 2
    new_k = lambda g: kn_sc[pl.ds(pl.multiple_of(g * qb, qb), qb), :]
    if n_earlier == 1:
        run_unrolled([((lambda: kn_sc[0:qb, :]), (lambda: vnt_sc[0]), 1, None)])
    elif n_earlier:
        run_loop(n_earlier, 0, new_k, lambda g: vnt_sc[g])

    rem_tiles = []
    for j in reversed(range(n_prem_blocks)):
        valid, rows = (last_valid, last_rows) if j == n_prem_blocks - 1 else (kb, kb)
        mask = None if valid == rows else lax.broadcasted_iota(jnp.int32, (rows, qb), 0) < valid
        rem_tiles += [((lambda j=j, rows=rows: kr_sc[j, 0:rows, :]),
                       (lambda j=j, rows=rows: vrt_sc[j, :, 0:rows]), qi, mask) for qi in range(n_q)]
    if rem_tiles:
        run_unrolled(rem_tiles)
    n_ptiles = n_q * n_pfull
    past_k = lambda g: kp_sc[pl.ds(pl.multiple_of(g * pg, pg), pg), :]
    if n_ptiles == 1:
        run_unrolled([((lambda: kp_sc[0:pg, :]), (lambda: vpt_sc[0]), 0, None)])
    elif n_ptiles:
        run_loop(n_ptiles, n_earlier, past_k, lambda g: vpt_sc[g])

    for g in range(n_q):
        o_ref[0, g * qb:(g + 1) * qb, :] = acc_sc[g].T.astype(o_ref.dtype)


def _stick_breaking(q_arr, q_col, k_arr, v_arr, past_k, past_v, past_layer, n_past_valid):
    bsz, tq, _ = q_arr.shape
    p = past_k.shape[2]
    qb = 2 * SB_BLOCK if tq % (2 * SB_BLOCK) == 0 else SB_BLOCK
    n_q = tq // qb
    pg = 2 * SB_BLOCK
    n_pfull = n_past_valid // pg
    n_prem_blocks = -(-(n_past_valid - n_pfull * pg) // SB_BLOCK)
    assert k_arr.shape[1] == tq and p % SB_BLOCK == 0 and n_past_valid <= p
    tiles = [(i, g) for i in range(n_q) for g in reversed(range(i))]
    tiles += [(i, g) for i in range(n_q) for g in reversed(range(n_pfull))]
    tiles = tiles or [(0, 0)]
    itab = jnp.asarray([t[0] for t in tiles], jnp.int32)
    gtab = jnp.asarray([t[1] for t in tiles], jnp.int32)
    past_map = ((lambda b, h, it, gt: (past_layer, b, 0, h)) if past_k.shape[1] == bsz
                else (lambda b, h, it, gt: (past_layer, 0, 0, h)))
    grid_spec = pltpu.PrefetchScalarGridSpec(
        num_scalar_prefetch=2,
        grid=(bsz, H_SB),
        in_specs=[
            pl.BlockSpec((1, tq, D_SB), lambda b, h, it, gt: (b, 0, q_col // D_SB + h)),
            pl.BlockSpec((1, tq, D_SB), lambda b, h, it, gt: (b, 0, h)),
            pl.BlockSpec((1, tq, D_SB), lambda b, h, it, gt: (b, 0, h)),
            pl.BlockSpec((None, 1, p, D_SB), past_map),
            pl.BlockSpec((None, 1, p, D_SB), past_map),
        ],
        out_specs=pl.BlockSpec((1, tq, D_SB), lambda b, h, it, gt: (b, 0, h)),
        scratch_shapes=[
            pltpu.VMEM((n_q, D_SB, qb), BF16),
            pltpu.VMEM((tq, D_SB), BF16),
            pltpu.VMEM((n_q, D_SB, qb), BF16),
            pltpu.VMEM((max(n_pfull, 1) * pg, D_SB), BF16),
            pltpu.VMEM((max(n_pfull, 1), D_SB, pg), BF16),
            pltpu.VMEM((max(n_prem_blocks, 1), SB_BLOCK, D_SB), BF16),
            pltpu.VMEM((max(n_prem_blocks, 1), D_SB, SB_BLOCK), BF16),
            pltpu.VMEM((n_q, D_SB, qb), F32),
            pltpu.VMEM((n_q, 1, qb), F32),
            pltpu.VMEM((2 * SB_UNROLL, max(qb, pg), qb), F32),
            pltpu.VMEM((2 * SB_UNROLL, 2 * max(qb, pg), qb), BF16),
            pltpu.VMEM((2 * SB_UNROLL, max(qb, pg), qb), F32),
        ])
    return pl.pallas_call(
        functools.partial(_stick_breaking_body, qb=qb, n_q=n_q, n_past_valid=n_past_valid),
        grid_spec=grid_spec,
        out_shape=jax.ShapeDtypeStruct((bsz, tq, H_SB * D_SB), BF16),
        compiler_params=_params("parallel", "parallel"),
        name="stick_breaking",
    )(itab, gtab, q_arr, k_arr, v_arr, past_k, past_v)


def _pool_body(u_ref, prev_ref, buf_ref, w_ref, sc_ref, o_ref, z_sc, *, tm, n_hist):
    i = pl.program_id(1)
    z_sc[POOL_HALO:POOL_HALO + tm, :] = u_ref[0]

    @pl.when(i == 0)
    def _():
        z_sc[0:POOL_HALO, :] = buf_ref[0]

    @pl.when(i > 0)
    def _():
        z_sc[0:POOL_HALO, :] = prev_ref[0]

    tr = min(tm, 256)
    for g, win in enumerate(POOL_WINDOWS):
        cols = slice(g * DG_POOL, (g + 1) * DG_POOL)
        for r0 in range(0, tm, tr):
            base = POOL_HALO + r0
            wsum = z_sc[base:base + tr, cols]
            for back in range(1, win):
                wsum = wsum + z_sc[base - back:base - back + tr, cols]
            t_idx = i * tm + r0 + lax.broadcasted_iota(jnp.int32, (tr, 1), 0)
            cnt = jnp.minimum(t_idx + (n_hist + 1), win).astype(F32)
            pooled = wsum / cnt - z_sc[base:base + tr, cols]
            mixed = jnp.dot(pooled.astype(BF16), w_ref[g], preferred_element_type=F32)
            o_ref[0, r0:r0 + tr, cols] = (mixed * sc_ref[:, cols]).astype(o_ref.dtype)


def _pool_mixer(proj3, hist, n_hist, mix_w, scale):
    bsz, t, _ = proj3.shape
    tm = min(t, 512)
    hpb = tm // POOL_HALO
    ucol = COL_U // D_MODEL
    hist_map = (lambda b, i: (b, 0, 0)) if hist.shape[0] == bsz else (lambda b, i: (0, 0, 0))
    return pl.pallas_call(
        functools.partial(_pool_body, tm=tm, n_hist=n_hist),
        grid=(bsz, t // tm),
        in_specs=[
            pl.BlockSpec((1, tm, D_MODEL), lambda b, i: (b, i, ucol)),
            pl.BlockSpec((1, POOL_HALO, D_MODEL),
                         lambda b, i: (b, jnp.maximum(i * hpb - 1, 0), ucol)),
            pl.BlockSpec((1, POOL_HALO, D_MODEL), hist_map),
            pl.BlockSpec((len(POOL_WINDOWS), DG_POOL, DG_POOL), lambda b, i: (0, 0, 0)),
            pl.BlockSpec((1, D_MODEL), lambda b, i: (0, 0)),
        ],
        out_specs=pl.BlockSpec((1, tm, D_MODEL), lambda b, i: (b, i, 0)),
        out_shape=jax.ShapeDtypeStruct((bsz, t, D_MODEL), BF16),
        scratch_shapes=[pltpu.VMEM((POOL_HALO + tm, D_MODEL), F32)],
        compiler_params=_params("parallel", "arbitrary"),
        name="pool_mixer",
    )(proj3, proj3, hist, mix_w, scale.reshape(1, D_MODEL))


def _merge_body(or_ref, os_ref, op_ref, g0_ref, g1_ref, g2_ref, h_ref, wb_ref, wo_ref, g_ref, b_ref,
                o_ref):
    mix = jax.nn.sigmoid(g0_ref[...]) * jnp.dot(or_ref[...], wb_ref[0], preferred_element_type=F32)
    mix = mix + jax.nn.sigmoid(g1_ref[...]) * jnp.dot(os_ref[...], wb_ref[1],
                                                      preferred_element_type=F32)
    mix = mix + jax.nn.sigmoid(g2_ref[...]) * jnp.dot(op_ref[...], wb_ref[2],
                                                      preferred_element_type=F32)
    out = jnp.dot(mix.astype(BF16), wo_ref[...], preferred_element_type=F32)
    o_ref[...] = _layer_norm(ALPHA * h_ref[...] + out, g_ref[...], b_ref[...])


def _merge(o_r, o_s, o_p, proj, h, w_branch, w_out, g, b):
    m = h.shape[0]
    tm = min(m, 512)
    row = lambda i: (i, 0)
    gcol = COL_GATES // D_MODEL
    return pl.pallas_call(
        _merge_body,
        grid=(m // tm,),
        in_specs=[
            pl.BlockSpec((tm, D_MODEL), row),
            pl.BlockSpec((tm, D_MODEL), row),
            pl.BlockSpec((tm, D_MODEL), row),
            pl.BlockSpec((tm, D_MODEL), lambda i: (i, gcol)),
            pl.BlockSpec((tm, D_MODEL), lambda i: (i, gcol + 1)),
            pl.BlockSpec((tm, D_MODEL), lambda i: (i, gcol + 2)),
            pl.BlockSpec((tm, D_MODEL), row),
            pl.BlockSpec((3, D_MODEL, D_MODEL), lambda i: (0, 0, 0)),
            pl.BlockSpec((D_MODEL, D_MODEL), lambda i: (0, 0)),
            pl.BlockSpec((1, D_MODEL), lambda i: (0, 0)),
            pl.BlockSpec((1, D_MODEL), lambda i: (0, 0)),
        ],
        out_specs=pl.BlockSpec((tm, D_MODEL), row),
        out_shape=jax.ShapeDtypeStruct((m, D_MODEL), F32),
        compiler_params=_params("parallel"),
        name="merge",
    )(o_r, o_s, o_p, proj, proj, proj, h, w_branch, w_out, g.reshape(1, D_MODEL),
      b.reshape(1, D_MODEL))


def _pad_rows(a, rows):
    return jnp.pad(a, ((0, 0), (0, rows - a.shape[1]), (0, 0)))


def _trunk_layer(h, past_k, past_v, past_layer, n_past, prefix_kv, s0, pool_buf, pos0, lw, layer,
                 kv_all):
    bsz, t, _ = h.shape
    m = bsz * t
    h1, h1b = _ffn_ln(h.reshape(m, D_MODEL), lw["up1"], lw["down1"], lw["ln_g"][0], lw["ln_b"][0],
                      emit_bf16=True)
    n_prefix = 0 if prefix_kv is None else prefix_kv[0].shape[1]
    proj, (k_all, v_all), kb, vb = _in_proj(h1b, lw["w_in"], bsz, t, n_prefix, layer, kv_all)
    proj3 = proj.reshape(bsz, t, D_PROJ)
    if prefix_kv is not None:
        front = lambda a: jnp.broadcast_to(a, (bsz,) + a.shape[1:])[None]
        k_all = lax.dynamic_update_slice(k_all, front(prefix_kv[0]), (layer, 0, 0, 0, 0))
        v_all = lax.dynamic_update_slice(v_all, front(prefix_kv[1]), (layer, 0, 0, 0, 0))

    o_r, s_new = _retention(proj3, pos0, s0, lw["ret_g"])

    if t % SB_BLOCK == 0:
        o_s = _stick_breaking(proj3, COL_QS, kb, vb, past_k, past_v, past_layer, n_past)
    else:
        q = _pad_rows(proj3[:, :, COL_QS:COL_QS + D_MODEL], SB_BLOCK)
        o_s = _stick_breaking(q, 0, _pad_rows(kb, SB_BLOCK), _pad_rows(vb, SB_BLOCK), past_k, past_v,
                              past_layer, n_past)[:, :t]

    n_hist = pool_buf.shape[1]
    hist = jnp.pad(pool_buf, ((0, 0), (POOL_HALO - n_hist, 0), (0, 0)))
    o_p = _pool_mixer(proj3, hist, n_hist, lw["pool_w"], lw["pool_scale"])

    h2 = _merge(o_r.reshape(m, D_MODEL), o_s.reshape(m, D_MODEL), o_p.reshape(m, D_MODEL), proj, h1,
                lw["w_branch"], lw["w_out"], lw["ln_g"][1], lw["ln_b"][1])
    h3 = _ffn_ln(h2, lw["up2"], lw["down2"], lw["ln_g"][2], lw["ln_b"][2], emit_bf16=False)

    u = proj3[:, :, COL_U:COL_U + D_MODEL]
    assert t >= POOL_BUF
    buf_new = u[:, t - POOL_BUF:]
    return h3.reshape(bsz, t, D_MODEL), ((k_all, v_all), s_new, buf_new)


def _layer_weights(l, w_in, ret_norm_g, pool_mix_w, pool_scale, w_branch, w_out,
                   ffn1_up, ffn1_down, ffn2_up, ffn2_down, ln_g, ln_b):
    return dict(
        w_in=w_in[l].astype(BF16), ret_g=ret_norm_g[l], pool_w=pool_mix_w[l].astype(BF16),
        pool_scale=pool_scale[l], w_branch=w_branch[l].astype(BF16), w_out=w_out[l].astype(BF16),
        up1=ffn1_up[l].astype(BF16), down1=ffn1_down[l].astype(BF16),
        up2=ffn2_up[l].astype(BF16), down2=ffn2_down[l].astype(BF16),
        ln_g=ln_g[l], ln_b=ln_b[l])


def kernel(x_prompt, x_sample, cache_sb_k, cache_sb_v, state_ret, state_pool, meta_tokens, w_in, ret_norm_g, pool_mix_w, pool_scale, w_branch, w_out, ffn1_up, ffn1_down, ffn2_up, ffn2_down, ln_g, ln_b):
    bsz = x_prompt.shape[0]
    dec_b, past_len = cache_sb_k.shape[1], cache_sb_k.shape[2]
    assert past_len % SB_BLOCK == 0
    h_meta = meta_tokens[None]
    h_p = x_prompt
    h_s = x_sample
    cache_k = cache_sb_k.reshape(DEPTH, dec_b, past_len, D_MODEL)
    cache_v = cache_sb_v.reshape(DEPTH, dec_b, past_len, D_MODEL)
    no_past = jnp.zeros((1, 1, SB_BLOCK, D_MODEL), F32)
    outs = [[] for _ in range(4)]
    meta_kv = prompt_kv = sample_kv = None
    for l in range(DEPTH):
        lw = _layer_weights(l, w_in, ret_norm_g, pool_mix_w, pool_scale, w_branch, w_out,
                            ffn1_up, ffn1_down, ffn2_up, ffn2_down, ln_g, ln_b)
        h_meta, (meta_kv, ms, mbuf) = _trunk_layer(
            h_meta, no_past, no_past, 0, 0, None, jnp.zeros((1, H_RET, DK_RET, DV_RET), F32),
            jnp.zeros((1, 0, D_MODEL), F32), 0, lw, l, meta_kv)
        mk, mv = meta_kv[0][l], meta_kv[1][l]
        as_past = lambda a: _pad_rows(a.reshape(1, N_META, D_MODEL), SB_BLOCK)[None]
        h_p, (prompt_kv, ps, pbuf) = _trunk_layer(
            h_p, as_past(mk), as_past(mv), 0, N_META, (mk, mv), ms, mbuf, N_META, lw, l, prompt_kv)
        h_s, (sample_kv, ss, sbuf) = _trunk_layer(
            h_s, cache_k, cache_v, l, past_len, None, state_ret[l], state_pool[l],
            N_META + past_len, lw, l, sample_kv)
        outs[0].append(ps)
        outs[1].append(pbuf)
        outs[2].append(ss)
        outs[3].append(sbuf)
    new_ret_p, new_pool_p, new_ret_s, new_pool_s = (jnp.stack(o) for o in outs)
    return (h_p, h_s, prompt_kv[0], prompt_kv[1], new_ret_p, new_pool_p,
            sample_kv[0], sample_kv[1], new_ret_s, new_pool_s)
```

```python
import functools
import math

import jax
import jax.numpy as jnp
from jax import lax
from jax.experimental import pallas as pl
from jax.experimental.pallas import tpu as pltpu

D_MODEL = 1024
DEPTH = 2
N_META = 16
H_RET = 4
DK_RET = D_MODEL // 8
DV_RET = 2 * DK_RET
H_SB = 8
D_SB = D_MODEL // 8
RET_CHUNK = 256
POOL_WINDOWS = (2, 4, 8, 16)
DG_POOL = D_MODEL // len(POOL_WINDOWS)
POOL_BUF = max(POOL_WINDOWS) - 1
POOL_HALO = 16
D_FF = ((8 * D_MODEL // 3 + 127) // 128) * 128
D_IN = 10 * D_MODEL
ROPE_BASE = 10000.0
LN_EPS = 1e-5
LOG2E = 1.0 / math.log(2.0)
ALPHA = (2 * DEPTH) ** 0.25

W_COL_QR, W_COL_KR, W_COL_VR, W_COL_GR = 0, 512, 1024, 2048
W_COL_QS, W_COL_KS, W_COL_VS, W_COL_U = 3072, 4096, 5120, 6144
D_PROJ = D_IN - W_COL_U
COL_U, COL_GATES = 0, D_MODEL

SB_BLOCK = 128
SB_UNROLL = 2
BF16_ROWS = 16
VMEM_LIMIT = 48 * 1024 * 1024
FFN_VMEM_LIMIT = 56 * 1024 * 1024

F32 = jnp.float32
BF16 = jnp.bfloat16


def _params(*sem):
    return pltpu.CompilerParams(dimension_semantics=sem, vmem_limit_bytes=VMEM_LIMIT)


def _layer_norm(y, g, b):
    mu = jnp.mean(y, axis=-1, keepdims=True)
    d = y - mu
    var = jnp.mean(d * d, axis=-1, keepdims=True)
    return d * lax.rsqrt(var + LN_EPS) * g + b


def _silu(x):
    return x * jax.nn.sigmoid(x)


def _ffn_ln_body(x_ref, wg_ref, wu_ref, wd_ref, g_ref, b_ref, o_ref, *maybe_ob_ref):
    x = x_ref[...]
    xb = x.astype(BF16)
    gate = jnp.dot(xb, wg_ref[...], preferred_element_type=F32)
    up = jnp.dot(xb, wu_ref[...], preferred_element_type=F32)
    act = (_silu(gate) * up).astype(BF16)
    ffn = jnp.dot(act, wd_ref[...], preferred_element_type=F32)
    h = _layer_norm(ALPHA * x + 0.5 * ffn, g_ref[...], b_ref[...])
    o_ref[...] = h
    for ob_ref in maybe_ob_ref:
        ob_ref[...] = h.astype(BF16)


def _ffn_ln(x, w_up, w_down, g, b, emit_bf16):
    m = x.shape[0]
    tm = min(m, 512)
    assert m % tm == 0
    row = lambda i: (i, 0)
    once = pl.Buffered(1)
    out_shape = [jax.ShapeDtypeStruct((m, D_MODEL), F32)]
    out_specs = [pl.BlockSpec((tm, D_MODEL), row)]
    if emit_bf16:
        out_shape.append(jax.ShapeDtypeStruct((m, D_MODEL), BF16))
        out_specs.append(pl.BlockSpec((tm, D_MODEL), row))
    res = pl.pallas_call(
        _ffn_ln_body,
        grid=(m // tm,),
        in_specs=[
            pl.BlockSpec((tm, D_MODEL), row),
            pl.BlockSpec((D_MODEL, D_FF), lambda i: (0, 0), pipeline_mode=once),
            pl.BlockSpec((D_MODEL, D_FF), lambda i: (0, 1), pipeline_mode=once),
            pl.BlockSpec((D_FF, D_MODEL), lambda i: (0, 0), pipeline_mode=once),
            pl.BlockSpec((1, D_MODEL), lambda i: (0, 0)),
            pl.BlockSpec((1, D_MODEL), lambda i: (0, 0)),
        ],
        out_specs=out_specs,
        out_shape=out_shape,
        compiler_params=pltpu.CompilerParams(dimension_semantics=("parallel",),
                                             vmem_limit_bytes=FFN_VMEM_LIMIT),
        name="ffn_ln",
    )(x, w_up, w_up, w_down, g.reshape(1, D_MODEL), b.reshape(1, D_MODEL))
    return res if emit_bf16 else res[0]


def _in_proj_body(x_ref, w_ref, *rest):
    proj_ref, qk_ref, vr_ref, gr_ref, qs_ref, kb_ref, vb_ref, kf_ref, vf_ref = rest[-9:]
    x = x_ref[...]

    def cols(c0, n):
        return jnp.dot(x, w_ref[:, c0:c0 + n], preferred_element_type=F32)

    for c0 in range(0, D_PROJ, D_MODEL):
        proj_ref[:, c0:c0 + D_MODEL] = cols(W_COL_U + c0, D_MODEL)
    qr, kr = cols(W_COL_QR, H_RET * DK_RET), cols(W_COL_KR, H_RET * DK_RET)
    vr, gr = cols(W_COL_VR, H_RET * DV_RET), cols(W_COL_GR, H_RET * DV_RET)
    for h in range(H_RET):
        qk_ref[0, h, :, 0:DK_RET] = qr[:, h * DK_RET:(h + 1) * DK_RET]
        qk_ref[0, h, :, DK_RET:2 * DK_RET] = kr[:, h * DK_RET:(h + 1) * DK_RET]
        vr_ref[0, h] = vr[:, h * DV_RET:(h + 1) * DV_RET].astype(BF16)
        gr_ref[0, h] = gr[:, h * DV_RET:(h + 1) * DV_RET]
    qs, k, v = cols(W_COL_QS, D_MODEL), cols(W_COL_KS, D_MODEL), cols(W_COL_VS, D_MODEL)
    for h in range(H_SB):
        head = slice(h * D_SB, (h + 1) * D_SB)
        qs_ref[0, h] = qs[:, head].astype(BF16)
        kb_ref[0, h] = k[:, head].astype(BF16)
        vb_ref[0, h] = v[:, head].astype(BF16)
    kf_ref[0, 0] = k.reshape(k.shape[0], H_SB, D_SB)
    vf_ref[0, 0] = v.reshape(v.shape[0], H_SB, D_SB)


def _in_proj(xb, w_in, bsz, t, row0, layer, kv_all):
    tm = min(t, 256)
    assert t % tm == 0
    nt = t // tm
    per_head = lambda n_heads, width: pl.BlockSpec((1, n_heads, tm, width), lambda b, i: (b, 0, i, 0))
    if row0:
        full = pl.BlockSpec((pl.Element(1), pl.Element(1), pl.Element(tm), pl.Element(H_SB),
                             pl.Element(D_SB)), lambda b, i: (layer, b, row0 + i * tm, 0, 0))
    else:
        full = pl.BlockSpec((1, 1, tm, H_SB, D_SB), lambda b, i: (layer, b, i, 0, 0))
    in_specs = [
        pl.BlockSpec((tm, D_MODEL), lambda b, i: (b * nt + i, 0)),
        pl.BlockSpec((D_MODEL, D_IN), lambda b, i: (0, 0), pipeline_mode=pl.Buffered(1)),
    ]
    operands = [xb, w_in]
    aliases = {}
    if kv_all is not None:
        in_specs += [pl.BlockSpec(memory_space=pl.ANY)] * 2
        operands += list(kv_all)
        aliases = {2: 7, 3: 8}
    res = pl.pallas_call(
        _in_proj_body,
        grid=(bsz, nt),
        in_specs=in_specs,
        out_specs=[pl.BlockSpec((tm, D_PROJ), lambda b, i: (b * nt + i, 0)),
                   per_head(H_RET, 2 * DK_RET), per_head(H_RET, DV_RET), per_head(H_RET, DV_RET),
                   per_head(H_SB, D_SB), per_head(H_SB, D_SB), per_head(H_SB, D_SB), full, full],
        out_shape=[jax.ShapeDtypeStruct((bsz * t, D_PROJ), F32),
                   jax.ShapeDtypeStruct((bsz, H_RET, t, 2 * DK_RET), F32),
                   jax.ShapeDtypeStruct((bsz, H_RET, t, DV_RET), BF16),
                   jax.ShapeDtypeStruct((bsz, H_RET, t, DV_RET), F32)]
        + [jax.ShapeDtypeStruct((bsz, H_SB, t, D_SB), BF16)] * 3
        + [jax.ShapeDtypeStruct((DEPTH, bsz, row0 + t, H_SB, D_SB), F32)] * 2,
        input_output_aliases=aliases,
        compiler_params=_params("parallel", "arbitrary"),
        name="in_proj",
    )(*operands)
    proj, qk_r, v_r, g_r, q_s, k_s, v_s, k_all, v_all = res
    return proj, (qk_r, v_r, g_r), (q_s, k_s, v_s), (k_all, v_all)


def _retention_body(qk_ref, v_ref, gr_ref, cos_ref, sin_ref, s0_ref, dintra_ref, dq_ref,
                    dk_ref, dc_ref, rg_ref, o_ref, snew_ref, qr_sc, kr_sc, kd_sc, s_sc,
                    *, chunk, n_chunks):
    d_intra = dintra_ref[0]
    d_q = dq_ref[0]
    d_k = dk_ref[0]
    d_c = dc_ref[0]
    rg = rg_ref[0]

    def rot(x, cos, sin):
        return x * cos + pltpu.roll(x, DK_RET // 2, 1) * sin

    for c in range(n_chunks):
        rows = slice(c * chunk, (c + 1) * chunk)
        cos = cos_ref[rows, :]
        sin = sin_ref[rows, :]
        k = rot(qk_ref[rows, DK_RET:2 * DK_RET], cos, sin) * (DK_RET ** -0.5)
        qr_sc[rows, :] = rot(qk_ref[rows, 0:DK_RET], cos, sin).astype(BF16)
        kr_sc[rows, :] = k.astype(BF16)
        kd_sc[rows, :] = (k * d_k).astype(BF16)

    s = s0_ref[0, 0]
    for c in range(n_chunks):
        rows = slice(c * chunk, (c + 1) * chunk)
        s_sc[c] = s.astype(BF16)
        s = s * d_c + lax.dot_general(kd_sc[rows, :], v_ref[rows, :], (((0,), (0,)), ((), ())),
                                      preferred_element_type=F32)
    snew_ref[0, 0] = s

    def products(c):
        rows = slice(c * chunk, (c + 1) * chunk)
        qc = qr_sc[rows, :]
        att = lax.dot_general(qc, kr_sc[rows, :], (((1,), (1,)), ((), ())),
                              preferred_element_type=F32)
        return att, jnp.dot(qc, s_sc[c], preferred_element_type=F32)

    nxt = products(0)
    for c in range(n_chunks):
        rows = slice(c * chunk, (c + 1) * chunk)
        att, inter = nxt
        if c + 1 < n_chunks:
            nxt = products(c + 1)
        o = (jnp.dot((att * d_intra).astype(BF16), v_ref[rows, :], preferred_element_type=F32)
             + inter * d_q)
        mu = jnp.mean(o, axis=-1, keepdims=True)
        d = o - mu
        var = jnp.mean(d * d, axis=-1, keepdims=True)
        on = d * lax.rsqrt(var + LN_EPS) * rg
        o_ref[0, rows, :] = (on * _silu(gr_ref[rows, :])).astype(o_ref.dtype)


def _retention_decays(chunk):
    lg = jnp.log1p(-jnp.exp(jnp.linspace(math.log(1.0 / 32), math.log(1.0 / 512), H_RET, dtype=F32)))
    i = jnp.arange(chunk, dtype=F32)
    diff = i[:, None] - i[None, :]
    d_intra = jnp.where(diff >= 0, jnp.exp(lg[:, None, None] * jnp.maximum(diff, 0.0)), 0.0)
    d_q = jnp.exp(lg[:, None] * (i[None, :] + 1.0))
    d_k = jnp.exp(lg[:, None] * (chunk - 1.0 - i[None, :]))
    d_c = jnp.exp(lg * chunk)
    return (d_intra,
            jnp.broadcast_to(d_q[:, :, None], (H_RET, chunk, DV_RET)),
            jnp.broadcast_to(d_k[:, :, None], (H_RET, chunk, DK_RET)),
            jnp.broadcast_to(d_c[:, None, None], (H_RET, 1, DV_RET)))


def _rotary_tables(pos0, t):
    half = DK_RET // 2
    inv_freq = ROPE_BASE ** (-jnp.arange(half, dtype=F32) / half)
    ang = (pos0 + jnp.arange(t, dtype=F32))[:, None] * inv_freq[None, :]
    cos = jnp.cos(ang)
    sin = jnp.sin(ang)
    return jnp.concatenate([cos, cos], axis=-1), jnp.concatenate([-sin, sin], axis=-1)


def _retention(qk_r, v_r, g_r, pos0, s0, ret_g):
    bsz, _, t, _ = qk_r.shape
    chunk = min(RET_CHUNK, t)
    n_chunks = t // chunk
    cos, sin = _rotary_tables(pos0, t)
    d_intra, d_q, d_k, d_c = _retention_decays(chunk)
    s0_map = (lambda b, h: (b, h, 0, 0)) if s0.shape[0] == bsz else (lambda b, h: (0, h, 0, 0))
    per_head = lambda b, h: (h, 0, 0)
    return pl.pallas_call(
        functools.partial(_retention_body, chunk=chunk, n_chunks=n_chunks),
        grid=(bsz, H_RET),
        in_specs=[
            pl.BlockSpec((None, None, t, 2 * DK_RET), lambda b, h: (b, h, 0, 0)),
            pl.BlockSpec((None, None, t, DV_RET), lambda b, h: (b, h, 0, 0)),
            pl.BlockSpec((None, None, t, DV_RET), lambda b, h: (b, h, 0, 0)),
            pl.BlockSpec((t, DK_RET), lambda b, h: (0, 0)),
            pl.BlockSpec((t, DK_RET), lambda b, h: (0, 0)),
            pl.BlockSpec((1, 1, DK_RET, DV_RET), s0_map),
            pl.BlockSpec((1, chunk, chunk), per_head),
            pl.BlockSpec((1, chunk, DV_RET), per_head),
            pl.BlockSpec((1, chunk, DK_RET), per_head),
            pl.BlockSpec((1, 1, DV_RET), per_head),
            pl.BlockSpec((1, 1, DV_RET), per_head),
        ],
        out_specs=[
            pl.BlockSpec((1, t, DV_RET), lambda b, h: (b, 0, h)),
            pl.BlockSpec((1, 1, DK_RET, DV_RET), lambda b, h: (b, h, 0, 0)),
        ],
        out_shape=[
            jax.ShapeDtypeStruct((bsz, t, H_RET * DV_RET), BF16),
            jax.ShapeDtypeStruct((bsz, H_RET, DK_RET, DV_RET), F32),
        ],
        scratch_shapes=[pltpu.VMEM((t, DK_RET), BF16), pltpu.VMEM((t, DK_RET), BF16),
                        pltpu.VMEM((t, DK_RET), BF16),
                        pltpu.VMEM((n_chunks, DK_RET, DV_RET), BF16)],
        compiler_params=_params("parallel", "parallel"),
        name="retention",
    )(qk_r, v_r, g_r, cos, sin, s0, d_intra, d_q, d_k, d_c, ret_g.reshape(H_RET, 1, DV_RET))


def _stick_breaking_body(itab_ref, gtab_ref, q_ref, kn_ref, vn_ref, kp_ref, vp_ref, o_ref,
                         qt_sc, kn_sc, vnt_sc, kp_sc, vpt_sc, kr_sc, vrt_sc, acc_sc, carry_sc,
                         z_sc, hilo_sc, incl_sc, *, qb, n_q, n_past_valid):
    kb = SB_BLOCK
    pg = 2 * kb
    n_pfull = n_past_valid // pg
    n_prem = n_past_valid - n_pfull * pg
    n_prem_blocks = -(-n_prem // kb)
    scale = D_SB ** -0.5

    transposed = lambda a: a.astype(F32).T.astype(BF16)
    kn_sc[...] = kn_ref[...].astype(BF16)
    for g in range(n_q):
        rows = slice(g * qb, (g + 1) * qb)
        vnt_sc[g] = transposed(vn_ref[rows, :])
        qt_sc[g] = transposed(q_ref[rows, :])
    if n_pfull:
        kp_sc[...] = kp_ref[0:n_pfull * pg, :].astype(BF16)
        for g in range(n_pfull):
            vpt_sc[g] = transposed(vp_ref[g * pg:(g + 1) * pg, :])
    for j in range(n_prem_blocks):
        rows = slice(n_pfull * pg + j * kb, n_pfull * pg + (j + 1) * kb)
        kr_sc[j] = kp_ref[rows, :].astype(BF16)
        vrt_sc[j] = transposed(vp_ref[rows, :])
    acc_sc[...] = jnp.zeros_like(acc_sc)
    carry_sc[...] = jnp.zeros_like(carry_sc)

    def suffix_ones2(sb):
        ones = (lax.broadcasted_iota(jnp.int32, (sb, sb), 1)
                >= lax.broadcasted_iota(jnp.int32, (sb, sb), 0)).astype(BF16)
        return jnp.concatenate([ones, ones], axis=1)

    last_valid = n_prem - (n_prem_blocks - 1) * kb if n_prem_blocks else kb
    last_rows = -(-last_valid // BF16_ROWS) * BF16_ROWS
    suffix_mats = {sb: suffix_ones2(sb) for sb in {kb, last_rows}}

    def stage_a(k_grp, q_t):
        return jnp.dot(k_grp, q_t, preferred_element_type=F32) * (scale * LOG2E)

    def stage_b(z, mask):
        neg_abs = lax.bitcast_convert_type(
            lax.bitcast_convert_type(z, jnp.uint32) | jnp.uint32(0x80000000), F32)
        go = jnp.maximum(z, 0.0) + jnp.log(1.0 + jnp.exp2(neg_abs)) * LOG2E
        if mask is not None:
            go = jnp.where(mask, go, 0.0)
        hi = go.astype(BF16)
        lo = (go - hi.astype(F32)).astype(BF16)
        sb = min(kb, z.shape[0])
        parts = []
        for s in range(z.shape[0] // sb):
            parts += [hi[s * sb:(s + 1) * sb], lo[s * sb:(s + 1) * sb]]
        return jnp.concatenate(parts, axis=0)

    def stage_c(hilo):
        sb = min(kb, hilo.shape[0] // 2)
        parts = [jnp.dot(suffix_mats[sb], hilo[s * 2 * sb:(s + 1) * 2 * sb],
                         preferred_element_type=F32) for s in range(hilo.shape[0] // (2 * sb))]
        return parts[0] if len(parts) == 1 else jnp.concatenate(parts, axis=0)

    def stage_de(z, incl, mask, vt_grp, qi):
        carry = carry_sc[qi]
        sb = min(kb, z.shape[0])
        n_sub = z.shape[0] // sb
        ws = [None] * n_sub
        for s in reversed(range(n_sub)):
            rows = slice(s * sb, (s + 1) * sb)
            w = jnp.exp2(z[rows] - (incl[rows] + carry))
            if mask is not None:
                w = jnp.where(mask[rows], w, 0.0)
            ws[s] = w.astype(BF16)
            carry = carry + incl[s * sb:s * sb + 1, :]
        carry_sc[qi] = carry
        w_all = ws[0] if n_sub == 1 else jnp.concatenate(ws, axis=0)
        acc_sc[qi] += jnp.dot(vt_grp, w_all, preferred_element_type=F32)

    def run_unrolled(tiles):
        n = len(tiles)
        z, hilo, incl = {}, {}, {}
        for tau in range(n + 2):
            if tau < n:
                k_fn, _, qi, mask = tiles[tau]
                z[tau] = stage_a(k_fn(), qt_sc[qi])
            if 1 <= tau <= n:
                incl[tau - 1] = stage_c(hilo.pop(tau - 1))
            if tau >= 2:
                _, vt_fn, qi2, mask2 = tiles[tau - 2]
                stage_de(z.pop(tau - 2), incl.pop(tau - 2), mask2, vt_fn(), qi2)
            if tau < n:
                hilo[tau] = stage_b(z[tau], mask)

    def run_loop(n, base, k_of, vt_of):
        u = SB_UNROLL if n % SB_UNROLL == 0 and n >= 2 * SB_UNROLL else 1
        n_steps = n // u
        tile = lambda step, k: base + step * u + k

        def a_all(step):
            return [stage_a(k_of(gtab_ref[tile(step, k)]), qt_sc[itab_ref[tile(step, k)]])
                    for k in range(u)]

        def c_all(slot):
            return [stage_c(hilo_sc[slot * u + k]) for k in range(u)]

        def b_store(zs, slot):
            for k in range(u):
                z_sc[slot * u + k] = zs[k]
                hilo_sc[slot * u + k] = stage_b(zs[k], None)

        def c_store(incs, slot):
            for k in range(u):
                incl_sc[slot * u + k] = incs[k]

        def finish(step, slot):
            for k in range(u):
                stage_de(z_sc[slot * u + k], incl_sc[slot * u + k], None,
                         vt_of(gtab_ref[tile(step, k)]), itab_ref[tile(step, k)])

        b_store(a_all(0), 0)
        zs = a_all(1)
        incs = c_all(0)
        b_store(zs, 1)
        c_store(incs, 0)

        def body(step, c):
            slot = step % 2
            zs = a_all(step)
            incs = c_all(1 - slot)
            finish(step - 2, slot)
            b_store(zs, slot)
            c_store(incs, 1 - slot)
            return c

        lax.fori_loop(2, n_steps, body, 0)
        slot = n_steps % 2
        incs = c_all(1 - slot)
        finish(n_steps - 2, slot)
        c_store(incs, 1 - slot)
        finish(n_steps - 1, 1 - slot)

    diag_mask = (lax.broadcasted_iota(jnp.int32, (qb, qb), 0)
                 < lax.broadcasted_iota(jnp.int32, (qb, qb), 1))
    run_unrolled([((lambda g=g: kn_sc[g * qb:(g + 1) * qb, :]), (lambda g=g: vnt_sc[g]), g, diag_mask)
                  for g in range(n_q)])
    n_earlier = n_q * (n_q - 1) // 2
    new_k = lambda g: kn_sc[pl.ds(pl.multiple_of(g * qb, qb), qb), :]
    if n_earlier == 1:
        run_unrolled([((lambda: kn_sc[0:qb, :]), (lambda: vnt_sc[0]), 1, None)])
    elif n_earlier:
        run_loop(n_earlier, 0, new_k, lambda g: vnt_sc[g])

    rem_tiles = []
    for j in reversed(range(n_prem_blocks)):
        valid, rows = (last_valid, last_rows) if j == n_prem_blocks - 1 else (kb, kb)
        mask = None if valid == rows else lax.broadcasted_iota(jnp.int32, (rows, qb), 0) < valid
        rem_tiles += [((lambda j=j, rows=rows: kr_sc[j, 0:rows, :]),
                       (lambda j=j, rows=rows: vrt_sc[j, :, 0:rows]), qi, mask) for qi in range(n_q)]
    if rem_tiles:
        run_unrolled(rem_tiles)
    n_ptiles = n_q * n_pfull
    past_k = lambda g: kp_sc[pl.ds(pl.multiple_of(g * pg, pg), pg), :]
    if n_ptiles == 1:
        run_unrolled([((lambda: kp_sc[0:pg, :]), (lambda: vpt_sc[0]), 0, None)])
    elif n_ptiles:
        run_loop(n_ptiles, n_earlier, past_k, lambda g: vpt_sc[g])

    for g in range(n_q):
        o_ref[0, g * qb:(g + 1) * qb, :] = acc_sc[g].T.astype(o_ref.dtype)


def _stick_breaking(q_arr, k_arr, v_arr, past_k, past_v, past_layer, n_past_valid):
    bsz, _, tq, _ = q_arr.shape
    p = past_k.shape[3]
    qb = 2 * SB_BLOCK if tq % (2 * SB_BLOCK) == 0 else SB_BLOCK
    n_q = tq // qb
    pg = 2 * SB_BLOCK
    n_pfull = n_past_valid // pg
    n_prem_blocks = -(-(n_past_valid - n_pfull * pg) // SB_BLOCK)
    assert k_arr.shape[2] == tq and p % SB_BLOCK == 0 and n_past_valid <= p
    tiles = [(i, g) for i in range(n_q) for g in reversed(range(i))]
    tiles += [(i, g) for i in range(n_q) for g in reversed(range(n_pfull))]
    tiles = tiles or [(0, 0)]
    itab = jnp.asarray([t[0] for t in tiles], jnp.int32)
    gtab = jnp.asarray([t[1] for t in tiles], jnp.int32)
    past_map = ((lambda b, h, it, gt: (past_layer, b, h, 0, 0)) if past_k.shape[1] == bsz
                else (lambda b, h, it, gt: (past_layer, 0, h, 0, 0)))
    per_head = pl.BlockSpec((None, None, tq, D_SB), lambda b, h, it, gt: (b, h, 0, 0))
    grid_spec = pltpu.PrefetchScalarGridSpec(
        num_scalar_prefetch=2,
        grid=(bsz, H_SB),
        in_specs=[
            per_head, per_head, per_head,
            pl.BlockSpec((None, None, None, p, D_SB), past_map),
            pl.BlockSpec((None, None, None, p, D_SB), past_map),
        ],
        out_specs=pl.BlockSpec((1, tq, D_SB), lambda b, h, it, gt: (b, 0, h)),
        scratch_shapes=[
            pltpu.VMEM((n_q, D_SB, qb), BF16),
            pltpu.VMEM((tq, D_SB), BF16),
            pltpu.VMEM((n_q, D_SB, qb), BF16),
            pltpu.VMEM((max(n_pfull, 1) * pg, D_SB), BF16),
            pltpu.VMEM((max(n_pfull, 1), D_SB, pg), BF16),
            pltpu.VMEM((max(n_prem_blocks, 1), SB_BLOCK, D_SB), BF16),
            pltpu.VMEM((max(n_prem_blocks, 1), D_SB, SB_BLOCK), BF16),
            pltpu.VMEM((n_q, D_SB, qb), F32),
            pltpu.VMEM((n_q, 1, qb), F32),
            pltpu.VMEM((2 * SB_UNROLL, max(qb, pg), qb), F32),
            pltpu.VMEM((2 * SB_UNROLL, 2 * max(qb, pg), qb), BF16),
            pltpu.VMEM((2 * SB_UNROLL, max(qb, pg), qb), F32),
        ])
    return pl.pallas_call(
        functools.partial(_stick_breaking_body, qb=qb, n_q=n_q, n_past_valid=n_past_valid),
        grid_spec=grid_spec,
        out_shape=jax.ShapeDtypeStruct((bsz, tq, H_SB * D_SB), BF16),
        compiler_params=_params("parallel", "parallel"),
        name="stick_breaking",
    )(itab, gtab, q_arr, k_arr, v_arr, past_k, past_v)


def _pool_body(u_ref, prev_ref, buf_ref, w_ref, sc_ref, o_ref, z_sc, *, tm, n_hist):
    i = pl.program_id(1)
    z_sc[POOL_HALO:POOL_HALO + tm, :] = u_ref[0]

    @pl.when(i == 0)
    def _():
        z_sc[0:POOL_HALO, :] = buf_ref[0]

    @pl.when(i > 0)
    def _():
        z_sc[0:POOL_HALO, :] = prev_ref[0]

    tr = min(tm, 256)
    for g, win in enumerate(POOL_WINDOWS):
        cols = slice(g * DG_POOL, (g + 1) * DG_POOL)
        for r0 in range(0, tm, tr):
            base = POOL_HALO + r0
            wsum = z_sc[base:base + tr, cols]
            for back in range(1, win):
                wsum = wsum + z_sc[base - back:base - back + tr, cols]
            t_idx = i * tm + r0 + lax.broadcasted_iota(jnp.int32, (tr, 1), 0)
            cnt = jnp.minimum(t_idx + (n_hist + 1), win).astype(F32)
            pooled = wsum / cnt - z_sc[base:base + tr, cols]
            mixed = jnp.dot(pooled.astype(BF16), w_ref[g], preferred_element_type=F32)
            o_ref[0, r0:r0 + tr, cols] = (mixed * sc_ref[:, cols]).astype(o_ref.dtype)


def _pool_mixer(proj3, hist, n_hist, mix_w, scale):
    bsz, t, _ = proj3.shape
    tm = min(t, 512)
    hpb = tm // POOL_HALO
    ucol = COL_U // D_MODEL
    hist_map = (lambda b, i: (b, 0, 0)) if hist.shape[0] == bsz else (lambda b, i: (0, 0, 0))
    return pl.pallas_call(
        functools.partial(_pool_body, tm=tm, n_hist=n_hist),
        grid=(bsz, t // tm),
        in_specs=[
            pl.BlockSpec((1, tm, D_MODEL), lambda b, i: (b, i, ucol)),
            pl.BlockSpec((1, POOL_HALO, D_MODEL),
                         lambda b, i: (b, jnp.maximum(i * hpb - 1, 0), ucol)),
            pl.BlockSpec((1, POOL_HALO, D_MODEL), hist_map),
            pl.BlockSpec((len(POOL_WINDOWS), DG_POOL, DG_POOL), lambda b, i: (0, 0, 0)),
            pl.BlockSpec((1, D_MODEL), lambda b, i: (0, 0)),
        ],
        out_specs=pl.BlockSpec((1, tm, D_MODEL), lambda b, i: (b, i, 0)),
        out_shape=jax.ShapeDtypeStruct((bsz, t, D_MODEL), BF16),
        scratch_shapes=[pltpu.VMEM((POOL_HALO + tm, D_MODEL), F32)],
        compiler_params=_params("parallel", "arbitrary"),
        name="pool_mixer",
    )(proj3, proj3, hist, mix_w, scale.reshape(1, D_MODEL))


def _merge_body(or_ref, os_ref, op_ref, g0_ref, g1_ref, g2_ref, h_ref, wb_ref, wo_ref, g_ref, b_ref,
                o_ref):
    mix = jax.nn.sigmoid(g0_ref[...]) * jnp.dot(or_ref[...], wb_ref[0], preferred_element_type=F32)
    mix = mix + jax.nn.sigmoid(g1_ref[...]) * jnp.dot(os_ref[...], wb_ref[1],
                                                      preferred_element_type=F32)
    mix = mix + jax.nn.sigmoid(g2_ref[...]) * jnp.dot(op_ref[...], wb_ref[2],
                                                      preferred_element_type=F32)
    out = jnp.dot(mix.astype(BF16), wo_ref[...], preferred_element_type=F32)
    o_ref[...] = _layer_norm(ALPHA * h_ref[...] + out, g_ref[...], b_ref[...])


def _merge(o_r, o_s, o_p, proj, h, w_branch, w_out, g, b):
    m = h.shape[0]
    tm = min(m, 512)
    row = lambda i: (i, 0)
    gcol = COL_GATES // D_MODEL
    return pl.pallas_call(
        _merge_body,
        grid=(m // tm,),
        in_specs=[
            pl.BlockSpec((tm, D_MODEL), row),
            pl.BlockSpec((tm, D_MODEL), row),
            pl.BlockSpec((tm, D_MODEL), row),
            pl.BlockSpec((tm, D_MODEL), lambda i: (i, gcol)),
            pl.BlockSpec((tm, D_MODEL), lambda i: (i, gcol + 1)),
            pl.BlockSpec((tm, D_MODEL), lambda i: (i, gcol + 2)),
            pl.BlockSpec((tm, D_MODEL), row),
            pl.BlockSpec((3, D_MODEL, D_MODEL), lambda i: (0, 0, 0)),
            pl.BlockSpec((D_MODEL, D_MODEL), lambda i: (0, 0)),
            pl.BlockSpec((1, D_MODEL), lambda i: (0, 0)),
            pl.BlockSpec((1, D_MODEL), lambda i: (0, 0)),
        ],
        out_specs=pl.BlockSpec((tm, D_MODEL), row),
        out_shape=jax.ShapeDtypeStruct((m, D_MODEL), F32),
        compiler_params=_params("parallel"),
        name="merge",
    )(o_r, o_s, o_p, proj, proj, proj, h, w_branch, w_out, g.reshape(1, D_MODEL),
      b.reshape(1, D_MODEL))


def _pad_rows(a, rows):
    pad = [(0, 0)] * a.ndim
    pad[-2] = (0, rows - a.shape[-2])
    return jnp.pad(a, pad)


def _trunk_layer(h, past_k, past_v, past_layer, n_past, prefix_kv, s0, pool_buf, pos0, lw, layer,
                 kv_all):
    bsz, t, _ = h.shape
    m = bsz * t
    h1, h1b = _ffn_ln(h.reshape(m, D_MODEL), lw["up1"], lw["down1"], lw["ln_g"][0], lw["ln_b"][0],
                      emit_bf16=True)
    n_prefix = 0 if prefix_kv is None else prefix_kv[0].shape[1]
    proj, ret_in, sb_in, (k_all, v_all) = _in_proj(h1b, lw["w_in"], bsz, t, n_prefix, layer, kv_all)
    proj3 = proj.reshape(bsz, t, D_PROJ)
    if prefix_kv is not None:
        front = lambda a: jnp.broadcast_to(a, (bsz,) + a.shape[1:])[None]
        k_all = lax.dynamic_update_slice(k_all, front(prefix_kv[0]), (layer, 0, 0, 0, 0))
        v_all = lax.dynamic_update_slice(v_all, front(prefix_kv[1]), (layer, 0, 0, 0, 0))

    o_r, s_new = _retention(*ret_in, pos0, s0, lw["ret_g"])

    if t % SB_BLOCK:
        sb_in = [_pad_rows(a, SB_BLOCK) for a in sb_in]
    o_s = _stick_breaking(*sb_in, past_k, past_v, past_layer, n_past)[:, :t]

    n_hist = pool_buf.shape[1]
    hist = jnp.pad(pool_buf, ((0, 0), (POOL_HALO - n_hist, 0), (0, 0)))
    o_p = _pool_mixer(proj3, hist, n_hist, lw["pool_w"], lw["pool_scale"])

    h2 = _merge(o_r.reshape(m, D_MODEL), o_s.reshape(m, D_MODEL), o_p.reshape(m, D_MODEL), proj, h1,
                lw["w_branch"], lw["w_out"], lw["ln_g"][1], lw["ln_b"][1])
    h3 = _ffn_ln(h2, lw["up2"], lw["down2"], lw["ln_g"][2], lw["ln_b"][2], emit_bf16=False)

    u = proj3[:, :, COL_U:COL_U + D_MODEL]
    assert t >= POOL_BUF
    buf_new = u[:, t - POOL_BUF:]
    return h3.reshape(bsz, t, D_MODEL), ((k_all, v_all), s_new, buf_new)


def _layer_weights(l, w_in, ret_norm_g, pool_mix_w, pool_scale, w_branch, w_out,
                   ffn1_up, ffn1_down, ffn2_up, ffn2_down, ln_g, ln_b):
    return dict(
        w_in=w_in[l].astype(BF16), ret_g=ret_norm_g[l], pool_w=pool_mix_w[l].astype(BF16),
        pool_scale=pool_scale[l], w_branch=w_branch[l].astype(BF16), w_out=w_out[l].astype(BF16),
        up1=ffn1_up[l].astype(BF16), down1=ffn1_down[l].astype(BF16),
        up2=ffn2_up[l].astype(BF16), down2=ffn2_down[l].astype(BF16),
        ln_g=ln_g[l], ln_b=ln_b[l])


def kernel(x_prompt, x_sample, cache_sb_k, cache_sb_v, state_ret, state_pool, meta_tokens, w_in, ret_norm_g, pool_mix_w, pool_scale, w_branch, w_out, ffn1_up, ffn1_down, ffn2_up, ffn2_down, ln_g, ln_b):
    bsz = x_prompt.shape[0]
    dec_b, past_len = cache_sb_k.shape[1], cache_sb_k.shape[2]
    assert past_len % SB_BLOCK == 0
    h_meta = meta_tokens[None]
    h_p = x_prompt
    h_s = x_sample
    cache_k = cache_sb_k.astype(BF16).transpose(0, 1, 3, 2, 4)
    cache_v = cache_sb_v.astype(BF16).transpose(0, 1, 3, 2, 4)
    no_past = jnp.zeros((1, 1, H_SB, SB_BLOCK, D_SB), BF16)
    outs = [[] for _ in range(4)]
    meta_kv = prompt_kv = sample_kv = None
    for l in range(DEPTH):
        lw = _layer_weights(l, w_in, ret_norm_g, pool_mix_w, pool_scale, w_branch, w_out,
                            ffn1_up, ffn1_down, ffn2_up, ffn2_down, ln_g, ln_b)
        h_meta, (meta_kv, ms, mbuf) = _trunk_layer(
            h_meta, no_past, no_past, 0, 0, None, jnp.zeros((1, H_RET, DK_RET, DV_RET), F32),
            jnp.zeros((1, 0, D_MODEL), F32), 0, lw, l, meta_kv)
        mk, mv = meta_kv[0][l], meta_kv[1][l]
        as_past = lambda a: _pad_rows(a.transpose(0, 2, 1, 3), SB_BLOCK)[None]
        h_p, (prompt_kv, ps, pbuf) = _trunk_layer(
            h_p, as_past(mk), as_past(mv), 0, N_META, (mk, mv), ms, mbuf, N_META, lw, l, prompt_kv)
        h_s, (sample_kv, ss, sbuf) = _trunk_layer(
            h_s, cache_k, cache_v, l, past_len, None, state_ret[l], state_pool[l],
            N_META + past_len, lw, l, sample_kv)
        outs[0].append(ps)
        outs[1].append(pbuf)
        outs[2].append(ss)
        outs[3].append(sbuf)
    new_ret_p, new_pool_p, new_ret_s, new_pool_s = (jnp.stack(o) for o in outs)
    return (h_p, h_s, prompt_kv[0], prompt_kv[1], new_ret_p, new_pool_p,
            sample_kv[0], sample_kv[1], new_ret_s, new_pool_s)
```

```python
import functools
import math

import jax
import jax.numpy as jnp
from jax import lax
from jax.experimental import pallas as pl
from jax.experimental.pallas import tpu as pltpu

D_MODEL = 1024
DEPTH = 2
N_META = 16
H_RET = 4
DK_RET = D_MODEL // 8
DV_RET = 2 * DK_RET
H_SB = 8
D_SB = D_MODEL // 8
RET_CHUNK = 256
POOL_WINDOWS = (2, 4, 8, 16)
DG_POOL = D_MODEL // len(POOL_WINDOWS)
POOL_BUF = max(POOL_WINDOWS) - 1
POOL_HALO = 16
D_FF = ((8 * D_MODEL // 3 + 127) // 128) * 128
D_IN = 10 * D_MODEL
ROPE_BASE = 10000.0
LN_EPS = 1e-5
LOG2E = 1.0 / math.log(2.0)
ALPHA = (2 * DEPTH) ** 0.25

W_COL_QR, W_COL_KR, W_COL_VR, W_COL_GR = 0, 512, 1024, 2048
W_COL_QS, W_COL_KS, W_COL_VS, W_COL_U, W_COL_GATES = 3072, 4096, 5120, 6144, 7168
D_PROJ = D_IN - W_COL_GATES

SB_BLOCK = 128
SB_UNROLL = 2
BF16_ROWS = 16
VMEM_LIMIT = 48 * 1024 * 1024
FFN_VMEM_LIMIT = 56 * 1024 * 1024

F32 = jnp.float32
BF16 = jnp.bfloat16


def _params(*sem):
    return pltpu.CompilerParams(dimension_semantics=sem, vmem_limit_bytes=VMEM_LIMIT)


def _layer_norm(y, g, b):
    mu = jnp.mean(y, axis=-1, keepdims=True)
    d = y - mu
    var = jnp.mean(d * d, axis=-1, keepdims=True)
    return d * lax.rsqrt(var + LN_EPS) * g + b


def _silu(x):
    return x * jax.nn.sigmoid(x)


def _ffn_ln_body(x_ref, wg_ref, wu_ref, wd_ref, g_ref, b_ref, o_ref, *maybe_ob_ref):
    x = x_ref[...]
    xb = x.astype(BF16)
    gate = jnp.dot(xb, wg_ref[...], preferred_element_type=F32)
    up = jnp.dot(xb, wu_ref[...], preferred_element_type=F32)
    act = (_silu(gate) * up).astype(BF16)
    ffn = jnp.dot(act, wd_ref[...], preferred_element_type=F32)
    h = _layer_norm(ALPHA * x + 0.5 * ffn, g_ref[...], b_ref[...])
    o_ref[...] = h
    for ob_ref in maybe_ob_ref:
        ob_ref[...] = h.astype(BF16)


def _ffn_ln(x, w_up, w_down, g, b, emit_bf16):
    m = x.shape[0]
    tm = min(m, 512)
    assert m % tm == 0
    row = lambda i: (i, 0)
    once = pl.Buffered(1)
    out_shape = [jax.ShapeDtypeStruct((m, D_MODEL), F32)]
    out_specs = [pl.BlockSpec((tm, D_MODEL), row)]
    if emit_bf16:
        out_shape.append(jax.ShapeDtypeStruct((m, D_MODEL), BF16))
        out_specs.append(pl.BlockSpec((tm, D_MODEL), row))
    res = pl.pallas_call(
        _ffn_ln_body,
        grid=(m // tm,),
        in_specs=[
            pl.BlockSpec((tm, D_MODEL), row),
            pl.BlockSpec((D_MODEL, D_FF), lambda i: (0, 0), pipeline_mode=once),
            pl.BlockSpec((D_MODEL, D_FF), lambda i: (0, 1), pipeline_mode=once),
            pl.BlockSpec((D_FF, D_MODEL), lambda i: (0, 0), pipeline_mode=once),
            pl.BlockSpec((1, D_MODEL), lambda i: (0, 0)),
            pl.BlockSpec((1, D_MODEL), lambda i: (0, 0)),
        ],
        out_specs=out_specs,
        out_shape=out_shape,
        compiler_params=pltpu.CompilerParams(dimension_semantics=("parallel",),
                                             vmem_limit_bytes=FFN_VMEM_LIMIT),
        name="ffn_ln",
    )(x, w_up, w_up, w_down, g.reshape(1, D_MODEL), b.reshape(1, D_MODEL))
    return res if emit_bf16 else res[0]


def _in_proj_body(x_ref, w_ref, hist_ref, pw_ref, psc_ref, *rest, tm, n_hist):
    (proj_ref, op_ref, tail_ref, qk_ref, vr_ref, gr_ref, qs_ref, kb_ref, vb_ref, kf_ref,
     vf_ref) = rest[-12:-1]
    z_sc = rest[-1]
    i = pl.program_id(1)
    x = x_ref[...]

    def cols(c0, n):
        return jnp.dot(x, w_ref[:, c0:c0 + n], preferred_element_type=F32)

    @pl.when(i == 0)
    def _():
        z_sc[0:POOL_HALO, :] = hist_ref[0]

    z_sc[POOL_HALO:POOL_HALO + tm, :] = cols(W_COL_U, D_MODEL)
    for c0 in range(0, D_PROJ, D_MODEL):
        proj_ref[:, c0:c0 + D_MODEL] = cols(W_COL_GATES + c0, D_MODEL)
    _pool_rows(z_sc, i * tm, tm, n_hist, pw_ref, psc_ref, op_ref)
    tail = z_sc[tm:tm + POOL_HALO, :]
    z_sc[0:POOL_HALO, :] = tail
    tail_ref[0] = tail

    qr, kr = cols(W_COL_QR, H_RET * DK_RET), cols(W_COL_KR, H_RET * DK_RET)
    vr, gr = cols(W_COL_VR, H_RET * DV_RET), cols(W_COL_GR, H_RET * DV_RET)
    for h in range(H_RET):
        qk_ref[0, h, :, 0:DK_RET] = qr[:, h * DK_RET:(h + 1) * DK_RET]
        qk_ref[0, h, :, DK_RET:2 * DK_RET] = kr[:, h * DK_RET:(h + 1) * DK_RET]
        vr_ref[0, h] = vr[:, h * DV_RET:(h + 1) * DV_RET].astype(BF16)
        gr_ref[0, h] = gr[:, h * DV_RET:(h + 1) * DV_RET]
    qs, k, v = cols(W_COL_QS, D_MODEL), cols(W_COL_KS, D_MODEL), cols(W_COL_VS, D_MODEL)
    for h in range(H_SB):
        head = slice(h * D_SB, (h + 1) * D_SB)
        qs_ref[0, h] = qs[:, head].astype(BF16)
        kb_ref[0, h] = k[:, head].astype(BF16)
        vb_ref[0, h] = v[:, head].astype(BF16)
    kf_ref[0, 0] = k.reshape(k.shape[0], H_SB, D_SB)
    vf_ref[0, 0] = v.reshape(v.shape[0], H_SB, D_SB)


def _in_proj(xb, w_in, bsz, t, row0, layer, kv_all, hist, n_hist, pool_w, pool_scale):
    tm = min(t, 256)
    assert t % tm == 0
    nt = t // tm
    per_head = lambda n_heads, width: pl.BlockSpec((1, n_heads, tm, width), lambda b, i: (b, 0, i, 0))
    if row0:
        full = pl.BlockSpec((pl.Element(1), pl.Element(1), pl.Element(tm), pl.Element(H_SB),
                             pl.Element(D_SB)), lambda b, i: (layer, b, row0 + i * tm, 0, 0))
    else:
        full = pl.BlockSpec((1, 1, tm, H_SB, D_SB), lambda b, i: (layer, b, i, 0, 0))
    hist_map = (lambda b, i: (b, 0, 0)) if hist.shape[0] == bsz else (lambda b, i: (0, 0, 0))
    in_specs = [
        pl.BlockSpec((tm, D_MODEL), lambda b, i: (b * nt + i, 0)),
        pl.BlockSpec((D_MODEL, D_IN), lambda b, i: (0, 0), pipeline_mode=pl.Buffered(1)),
        pl.BlockSpec((1, POOL_HALO, D_MODEL), hist_map),
        pl.BlockSpec((len(POOL_WINDOWS), DG_POOL, DG_POOL), lambda b, i: (0, 0, 0)),
        pl.BlockSpec((1, D_MODEL), lambda b, i: (0, 0)),
    ]
    operands = [xb, w_in, hist, pool_w, pool_scale.reshape(1, D_MODEL)]
    aliases = {}
    if kv_all is not None:
        in_specs += [pl.BlockSpec(memory_space=pl.ANY)] * 2
        operands += list(kv_all)
        aliases = {5: 9, 6: 10}
    rows = lambda width: pl.BlockSpec((tm, width), lambda b, i: (b * nt + i, 0))
    res = pl.pallas_call(
        functools.partial(_in_proj_body, tm=tm, n_hist=n_hist),
        grid=(bsz, nt),
        in_specs=in_specs,
        out_specs=[rows(D_PROJ), rows(D_MODEL),
                   pl.BlockSpec((1, POOL_HALO, D_MODEL), lambda b, i: (b, 0, 0)),
                   per_head(H_RET, 2 * DK_RET), per_head(H_RET, DV_RET), per_head(H_RET, DV_RET),
                   per_head(H_SB, D_SB), per_head(H_SB, D_SB), per_head(H_SB, D_SB), full, full],
        out_shape=[jax.ShapeDtypeStruct((bsz * t, D_PROJ), F32),
                   jax.ShapeDtypeStruct((bsz * t, D_MODEL), BF16),
                   jax.ShapeDtypeStruct((bsz, POOL_HALO, D_MODEL), F32),
                   jax.ShapeDtypeStruct((bsz, H_RET, t, 2 * DK_RET), F32),
                   jax.ShapeDtypeStruct((bsz, H_RET, t, DV_RET), BF16),
                   jax.ShapeDtypeStruct((bsz, H_RET, t, DV_RET), F32)]
        + [jax.ShapeDtypeStruct((bsz, H_SB, t, D_SB), BF16)] * 3
        + [jax.ShapeDtypeStruct((DEPTH, bsz, row0 + t, H_SB, D_SB), F32)] * 2,
        input_output_aliases=aliases,
        scratch_shapes=[pltpu.VMEM((POOL_HALO + tm, D_MODEL), F32)],
        compiler_params=_params("parallel", "arbitrary"),
        name="in_proj",
    )(*operands)
    proj, o_p, tail, qk_r, v_r, g_r, q_s, k_s, v_s, k_all, v_all = res
    return proj, (o_p, tail), (qk_r, v_r, g_r), (q_s, k_s, v_s), (k_all, v_all)


def _retention_body(qk_ref, v_ref, gr_ref, cos_ref, sin_ref, s0_ref, dintra_ref, dq_ref,
                    dk_ref, dc_ref, rg_ref, o_ref, snew_ref, qr_sc, kr_sc, kd_sc, s_sc,
                    *, chunk, n_chunks):
    d_intra = dintra_ref[0]
    d_q = dq_ref[0]
    d_k = dk_ref[0]
    d_c = dc_ref[0]
    rg = rg_ref[0]

    def rot(x, cos, sin):
        return x * cos + pltpu.roll(x, DK_RET // 2, 1) * sin

    for c in range(n_chunks):
        rows = slice(c * chunk, (c + 1) * chunk)
        cos = cos_ref[rows, :]
        sin = sin_ref[rows, :]
        k = rot(qk_ref[rows, DK_RET:2 * DK_RET], cos, sin) * (DK_RET ** -0.5)
        qr_sc[rows, :] = rot(qk_ref[rows, 0:DK_RET], cos, sin).astype(BF16)
        kr_sc[rows, :] = k.astype(BF16)
        kd_sc[rows, :] = (k * d_k).astype(BF16)

    s = s0_ref[0, 0]
    for c in range(n_chunks):
        rows = slice(c * chunk, (c + 1) * chunk)
        s_sc[c] = s.astype(BF16)
        s = s * d_c + lax.dot_general(kd_sc[rows, :], v_ref[rows, :], (((0,), (0,)), ((), ())),
                                      preferred_element_type=F32)
    snew_ref[0, 0] = s

    def products(c):
        rows = slice(c * chunk, (c + 1) * chunk)
        qc = qr_sc[rows, :]
        att = lax.dot_general(qc, kr_sc[rows, :], (((1,), (1,)), ((), ())),
                              preferred_element_type=F32)
        return att, jnp.dot(qc, s_sc[c], preferred_element_type=F32)

    nxt = products(0)
    for c in range(n_chunks):
        rows = slice(c * chunk, (c + 1) * chunk)
        att, inter = nxt
        if c + 1 < n_chunks:
            nxt = products(c + 1)
        o = (jnp.dot((att * d_intra).astype(BF16), v_ref[rows, :], preferred_element_type=F32)
             + inter * d_q)
        mu = jnp.mean(o, axis=-1, keepdims=True)
        d = o - mu
        var = jnp.mean(d * d, axis=-1, keepdims=True)
        on = d * lax.rsqrt(var + LN_EPS) * rg
        o_ref[0, rows, :] = (on * _silu(gr_ref[rows, :])).astype(o_ref.dtype)


def _retention_decays(chunk):
    lg = jnp.log1p(-jnp.exp(jnp.linspace(math.log(1.0 / 32), math.log(1.0 / 512), H_RET, dtype=F32)))
    i = jnp.arange(chunk, dtype=F32)
    diff = i[:, None] - i[None, :]
    d_intra = jnp.where(diff >= 0, jnp.exp(lg[:, None, None] * jnp.maximum(diff, 0.0)), 0.0)
    d_q = jnp.exp(lg[:, None] * (i[None, :] + 1.0))
    d_k = jnp.exp(lg[:, None] * (chunk - 1.0 - i[None, :]))
    d_c = jnp.exp(lg * chunk)
    return (d_intra,
            jnp.broadcast_to(d_q[:, :, None], (H_RET, chunk, DV_RET)),
            jnp.broadcast_to(d_k[:, :, None], (H_RET, chunk, DK_RET)),
            jnp.broadcast_to(d_c[:, None, None], (H_RET, 1, DV_RET)))


def _rotary_tables(pos0, t):
    half = DK_RET // 2
    inv_freq = ROPE_BASE ** (-jnp.arange(half, dtype=F32) / half)
    ang = (pos0 + jnp.arange(t, dtype=F32))[:, None] * inv_freq[None, :]
    cos = jnp.cos(ang)
    sin = jnp.sin(ang)
    return jnp.concatenate([cos, cos], axis=-1), jnp.concatenate([-sin, sin], axis=-1)


def _retention(qk_r, v_r, g_r, pos0, s0, ret_g):
    bsz, _, t, _ = qk_r.shape
    chunk = min(RET_CHUNK, t)
    n_chunks = t // chunk
    cos, sin = _rotary_tables(pos0, t)
    d_intra, d_q, d_k, d_c = _retention_decays(chunk)
    s0_map = (lambda b, h: (b, h, 0, 0)) if s0.shape[0] == bsz else (lambda b, h: (0, h, 0, 0))
    per_head = lambda b, h: (h, 0, 0)
    return pl.pallas_call(
        functools.partial(_retention_body, chunk=chunk, n_chunks=n_chunks),
        grid=(bsz, H_RET),
        in_specs=[
            pl.BlockSpec((None, None, t, 2 * DK_RET), lambda b, h: (b, h, 0, 0)),
            pl.BlockSpec((None, None, t, DV_RET), lambda b, h: (b, h, 0, 0)),
            pl.BlockSpec((None, None, t, DV_RET), lambda b, h: (b, h, 0, 0)),
            pl.BlockSpec((t, DK_RET), lambda b, h: (0, 0)),
            pl.BlockSpec((t, DK_RET), lambda b, h: (0, 0)),
            pl.BlockSpec((1, 1, DK_RET, DV_RET), s0_map),
            pl.BlockSpec((1, chunk, chunk), per_head),
            pl.BlockSpec((1, chunk, DV_RET), per_head),
            pl.BlockSpec((1, chunk, DK_RET), per_head),
            pl.BlockSpec((1, 1, DV_RET), per_head),
            pl.BlockSpec((1, 1, DV_RET), per_head),
        ],
        out_specs=[
            pl.BlockSpec((1, t, DV_RET), lambda b, h: (b, 0, h)),
            pl.BlockSpec((1, 1, DK_RET, DV_RET), lambda b, h: (b, h, 0, 0)),
        ],
        out_shape=[
            jax.ShapeDtypeStruct((bsz, t, H_RET * DV_RET), BF16),
            jax.ShapeDtypeStruct((bsz, H_RET, DK_RET, DV_RET), F32),
        ],
        scratch_shapes=[pltpu.VMEM((t, DK_RET), BF16), pltpu.VMEM((t, DK_RET), BF16),
                        pltpu.VMEM((t, DK_RET), BF16),
                        pltpu.VMEM((n_chunks, DK_RET, DV_RET), BF16)],
        compiler_params=_params("parallel", "parallel"),
        name="retention",
    )(qk_r, v_r, g_r, cos, sin, s0, d_intra, d_q, d_k, d_c, ret_g.reshape(H_RET, 1, DV_RET))


def _stick_breaking_body(itab_ref, gtab_ref, q_ref, kn_ref, vn_ref, kp_ref, vp_ref, o_ref,
                         qt_sc, kn_sc, vnt_sc, kp_sc, vpt_sc, kr_sc, vrt_sc, acc_sc, carry_sc,
                         z_sc, hilo_sc, incl_sc, *, qb, n_q, n_past_valid):
    kb = SB_BLOCK
    pg = 2 * kb
    n_pfull = n_past_valid // pg
    n_prem = n_past_valid - n_pfull * pg
    n_prem_blocks = -(-n_prem // kb)
    scale = D_SB ** -0.5

    transposed = lambda a: a.astype(F32).T.astype(BF16)
    kn_sc[...] = kn_ref[...].astype(BF16)
    for g in range(n_q):
        rows = slice(g * qb, (g + 1) * qb)
        vnt_sc[g] = transposed(vn_ref[rows, :])
        qt_sc[g] = transposed(q_ref[rows, :])
    if n_pfull:
        kp_sc[...] = kp_ref[0:n_pfull * pg, :].astype(BF16)
        for g in range(n_pfull):
            vpt_sc[g] = transposed(vp_ref[g * pg:(g + 1) * pg, :])
    for j in range(n_prem_blocks):
        rows = slice(n_pfull * pg + j * kb, n_pfull * pg + (j + 1) * kb)
        kr_sc[j] = kp_ref[rows, :].astype(BF16)
        vrt_sc[j] = transposed(vp_ref[rows, :])
    acc_sc[...] = jnp.zeros_like(acc_sc)
    carry_sc[...] = jnp.zeros_like(carry_sc)

    def suffix_ones2(sb):
        ones = (lax.broadcasted_iota(jnp.int32, (sb, sb), 1)
                >= lax.broadcasted_iota(jnp.int32, (sb, sb), 0)).astype(BF16)
        return jnp.concatenate([ones, ones], axis=1)

    last_valid = n_prem - (n_prem_blocks - 1) * kb if n_prem_blocks else kb
    last_rows = -(-last_valid // BF16_ROWS) * BF16_ROWS
    suffix_mats = {sb: suffix_ones2(sb) for sb in {kb, last_rows}}

    def stage_a(k_grp, q_t):
        return jnp.dot(k_grp, q_t, preferred_element_type=F32) * (scale * LOG2E)

    def stage_b(z, mask):
        neg_abs = lax.bitcast_convert_type(
            lax.bitcast_convert_type(z, jnp.uint32) | jnp.uint32(0x80000000), F32)
        go = jnp.maximum(z, 0.0) + jnp.log(1.0 + jnp.exp2(neg_abs)) * LOG2E
        if mask is not None:
            go = jnp.where(mask, go, 0.0)
        hi = go.astype(BF16)
        lo = (go - hi.astype(F32)).astype(BF16)
        sb = min(kb, z.shape[0])
        parts = []
        for s in range(z.shape[0] // sb):
            parts += [hi[s * sb:(s + 1) * sb], lo[s * sb:(s + 1) * sb]]
        return jnp.concatenate(parts, axis=0)

    def stage_c(hilo):
        sb = min(kb, hilo.shape[0] // 2)
        parts = [jnp.dot(suffix_mats[sb], hilo[s * 2 * sb:(s + 1) * 2 * sb],
                         preferred_element_type=F32) for s in range(hilo.shape[0] // (2 * sb))]
        return parts[0] if len(parts) == 1 else jnp.concatenate(parts, axis=0)

    def stage_de(z, incl, mask, vt_grp, qi):
        carry = carry_sc[qi]
        sb = min(kb, z.shape[0])
        n_sub = z.shape[0] // sb
        ws = [None] * n_sub
        for s in reversed(range(n_sub)):
            rows = slice(s * sb, (s + 1) * sb)
            w = jnp.exp2(z[rows] - (incl[rows] + carry))
            if mask is not None:
                w = jnp.where(mask[rows], w, 0.0)
            ws[s] = w.astype(BF16)
            carry = carry + incl[s * sb:s * sb + 1, :]
        carry_sc[qi] = carry
        w_all = ws[0] if n_sub == 1 else jnp.concatenate(ws, axis=0)
        acc_sc[qi] += jnp.dot(vt_grp, w_all, preferred_element_type=F32)

    def run_unrolled(tiles):
        n = len(tiles)
        z, hilo, incl = {}, {}, {}
        for tau in range(n + 2):
            if tau < n:
                k_fn, _, qi, mask = tiles[tau]
                z[tau] = stage_a(k_fn(), qt_sc[qi])
            if 1 <= tau <= n:
                incl[tau - 1] = stage_c(hilo.pop(tau - 1))
            if tau >= 2:
                _, vt_fn, qi2, mask2 = tiles[tau - 2]
                stage_de(z.pop(tau - 2), incl.pop(tau - 2), mask2, vt_fn(), qi2)
            if tau < n:
                hilo[tau] = stage_b(z[tau], mask)

    def run_loop(n, base, k_of, vt_of):
        u = SB_UNROLL if n % SB_UNROLL == 0 and n >= 2 * SB_UNROLL else 1
        n_steps = n // u
        tile = lambda step, k: base + step * u + k

        def a_all(step):
            return [stage_a(k_of(gtab_ref[tile(step, k)]), qt_sc[itab_ref[tile(step, k)]])
                    for k in range(u)]

        def c_all(slot):
            return [stage_c(hilo_sc[slot * u + k]) for k in range(u)]

        def b_store(zs, slot):
            for k in range(u):
                z_sc[slot * u + k] = zs[k]
                hilo_sc[slot * u + k] = stage_b(zs[k], None)

        def c_store(incs, slot):
            for k in range(u):
                incl_sc[slot * u + k] = incs[k]

        def finish(step, slot):
            for k in range(u):
                stage_de(z_sc[slot * u + k], incl_sc[slot * u + k], None,
                         vt_of(gtab_ref[tile(step, k)]), itab_ref[tile(step, k)])

        b_store(a_all(0), 0)
        zs = a_all(1)
        incs = c_all(0)
        b_store(zs, 1)
        c_store(incs, 0)

        def body(step, c):
            slot = step % 2
            zs = a_all(step)
            incs = c_all(1 - slot)
            finish(step - 2, slot)
            b_store(zs, slot)
            c_store(incs, 1 - slot)
            return c

        lax.fori_loop(2, n_steps, body, 0)
        slot = n_steps % 2
        incs = c_all(1 - slot)
        finish(n_steps - 2, slot)
        c_store(incs, 1 - slot)
        finish(n_steps - 1, 1 - slot)

    diag_mask = (lax.broadcasted_iota(jnp.int32, (qb, qb), 0)
                 < lax.broadcasted_iota(jnp.int32, (qb, qb), 1))
    run_unrolled([((lambda g=g: kn_sc[g * qb:(g + 1) * qb, :]), (lambda g=g: vnt_sc[g]), g, diag_mask)
                  for g in range(n_q)])
    n_earlier = n_q * (n_q - 1) // 2
    new_k = lambda g: kn_sc[pl.ds(pl.multiple_of(g * qb, qb), qb), :]
    if n_earlier == 1:
        run_unrolled([((lambda: kn_sc[0:qb, :]), (lambda: vnt_sc[0]), 1, None)])
    elif n_earlier:
        run_loop(n_earlier, 0, new_k, lambda g: vnt_sc[g])

    rem_tiles = []
    for j in reversed(range(n_prem_blocks)):
        valid, rows = (last_valid, last_rows) if j == n_prem_blocks - 1 else (kb, kb)
        mask = None if valid == rows else lax.broadcasted_iota(jnp.int32, (rows, qb), 0) < valid
        rem_tiles += [((lambda j=j, rows=rows: kr_sc[j, 0:rows, :]),
                       (lambda j=j, rows=rows: vrt_sc[j, :, 0:rows]), qi, mask) for qi in range(n_q)]
    if rem_tiles:
        run_unrolled(rem_tiles)
    n_ptiles = n_q * n_pfull
    past_k = lambda g: kp_sc[pl.ds(pl.multiple_of(g * pg, pg), pg), :]
    if n_ptiles == 1:
        run_unrolled([((lambda: kp_sc[0:pg, :]), (lambda: vpt_sc[0]), 0, None)])
    elif n_ptiles:
        run_loop(n_ptiles, n_earlier, past_k, lambda g: vpt_sc[g])

    for g in range(n_q):
        o_ref[0, g * qb:(g + 1) * qb, :] = acc_sc[g].T.astype(o_ref.dtype)


def _stick_breaking(q_arr, k_arr, v_arr, past_k, past_v, past_layer, n_past_valid):
    bsz, _, tq, _ = q_arr.shape
    p = past_k.shape[3]
    qb = 2 * SB_BLOCK if tq % (2 * SB_BLOCK) == 0 else SB_BLOCK
    n_q = tq // qb
    pg = 2 * SB_BLOCK
    n_pfull = n_past_valid // pg
    n_prem_blocks = -(-(n_past_valid - n_pfull * pg) // SB_BLOCK)
    assert k_arr.shape[2] == tq and p % SB_BLOCK == 0 and n_past_valid <= p
    tiles = [(i, g) for i in range(n_q) for g in reversed(range(i))]
    tiles += [(i, g) for i in range(n_q) for g in reversed(range(n_pfull))]
    tiles = tiles or [(0, 0)]
    itab = jnp.asarray([t[0] for t in tiles], jnp.int32)
    gtab = jnp.asarray([t[1] for t in tiles], jnp.int32)
    past_map = ((lambda b, h, it, gt: (past_layer, b, h, 0, 0)) if past_k.shape[1] == bsz
                else (lambda b, h, it, gt: (past_layer, 0, h, 0, 0)))
    per_head = pl.BlockSpec((None, None, tq, D_SB), lambda b, h, it, gt: (b, h, 0, 0))
    grid_spec = pltpu.PrefetchScalarGridSpec(
        num_scalar_prefetch=2,
        grid=(bsz, H_SB),
        in_specs=[
            per_head, per_head, per_head,
            pl.BlockSpec((None, None, None, p, D_SB), past_map),
            pl.BlockSpec((None, None, None, p, D_SB), past_map),
        ],
        out_specs=pl.BlockSpec((1, tq, D_SB), lambda b, h, it, gt: (b, 0, h)),
        scratch_shapes=[
            pltpu.VMEM((n_q, D_SB, qb), BF16),
            pltpu.VMEM((tq, D_SB), BF16),
            pltpu.VMEM((n_q, D_SB, qb), BF16),
            pltpu.VMEM((max(n_pfull, 1) * pg, D_SB), BF16),
            pltpu.VMEM((max(n_pfull, 1), D_SB, pg), BF16),
            pltpu.VMEM((max(n_prem_blocks, 1), SB_BLOCK, D_SB), BF16),
            pltpu.VMEM((max(n_prem_blocks, 1), D_SB, SB_BLOCK), BF16),
            pltpu.VMEM((n_q, D_SB, qb), F32),
            pltpu.VMEM((n_q, 1, qb), F32),
            pltpu.VMEM((2 * SB_UNROLL, max(qb, pg), qb), F32),
            pltpu.VMEM((2 * SB_UNROLL, 2 * max(qb, pg), qb), BF16),
            pltpu.VMEM((2 * SB_UNROLL, max(qb, pg), qb), F32),
        ])
    return pl.pallas_call(
        functools.partial(_stick_breaking_body, qb=qb, n_q=n_q, n_past_valid=n_past_valid),
        grid_spec=grid_spec,
        out_shape=jax.ShapeDtypeStruct((bsz, tq, H_SB * D_SB), BF16),
        compiler_params=_params("parallel", "parallel"),
        name="stick_breaking",
    )(itab, gtab, q_arr, k_arr, v_arr, past_k, past_v)


def _pool_rows(z_sc, first_row, tm, n_hist, w_ref, sc_ref, o_ref):
    tr = min(tm, 256)
    for g, win in enumerate(POOL_WINDOWS):
        cols = slice(g * DG_POOL, (g + 1) * DG_POOL)
        for r0 in range(0, tm, tr):
            base = POOL_HALO + r0
            wsum = z_sc[base:base + tr, cols]
            for back in range(1, win):
                wsum = wsum + z_sc[base - back:base - back + tr, cols]
            t_idx = first_row + r0 + lax.broadcasted_iota(jnp.int32, (tr, 1), 0)
            cnt = jnp.minimum(t_idx + (n_hist + 1), win).astype(F32)
            pooled = wsum / cnt - z_sc[base:base + tr, cols]
            mixed = jnp.dot(pooled.astype(BF16), w_ref[g], preferred_element_type=F32)
            o_ref[r0:r0 + tr, cols] = (mixed * sc_ref[:, cols]).astype(o_ref.dtype)


def _merge_body(or_ref, os_ref, op_ref, g0_ref, g1_ref, g2_ref, h_ref, wb_ref, wo_ref, g_ref, b_ref,
                o_ref):
    mix = jax.nn.sigmoid(g0_ref[...]) * jnp.dot(or_ref[...], wb_ref[0], preferred_element_type=F32)
    mix = mix + jax.nn.sigmoid(g1_ref[...]) * jnp.dot(os_ref[...], wb_ref[1],
                                                      preferred_element_type=F32)
    mix = mix + jax.nn.sigmoid(g2_ref[...]) * jnp.dot(op_ref[...], wb_ref[2],
                                                      preferred_element_type=F32)
    out = jnp.dot(mix.astype(BF16), wo_ref[...], preferred_element_type=F32)
    o_ref[...] = _layer_norm(ALPHA * h_ref[...] + out, g_ref[...], b_ref[...])


def _merge(o_r, o_s, o_p, proj, h, w_branch, w_out, g, b):
    m = h.shape[0]
    tm = min(m, 512)
    row = lambda i: (i, 0)
    gcol = 0
    return pl.pallas_call(
        _merge_body,
        grid=(m // tm,),
        in_specs=[
            pl.BlockSpec((tm, D_MODEL), row),
            pl.BlockSpec((tm, D_MODEL), row),
            pl.BlockSpec((tm, D_MODEL), row),
            pl.BlockSpec((tm, D_MODEL), lambda i: (i, gcol)),
            pl.BlockSpec((tm, D_MODEL), lambda i: (i, gcol + 1)),
            pl.BlockSpec((tm, D_MODEL), lambda i: (i, gcol + 2)),
            pl.BlockSpec((tm, D_MODEL), row),
            pl.BlockSpec((3, D_MODEL, D_MODEL), lambda i: (0, 0, 0)),
            pl.BlockSpec((D_MODEL, D_MODEL), lambda i: (0, 0)),
            pl.BlockSpec((1, D_MODEL), lambda i: (0, 0)),
            pl.BlockSpec((1, D_MODEL), lambda i: (0, 0)),
        ],
        out_specs=pl.BlockSpec((tm, D_MODEL), row),
        out_shape=jax.ShapeDtypeStruct((m, D_MODEL), F32),
        compiler_params=_params("parallel"),
        name="merge",
    )(o_r, o_s, o_p, proj, proj, proj, h, w_branch, w_out, g.reshape(1, D_MODEL),
      b.reshape(1, D_MODEL))


def _pad_rows(a, rows):
    pad = [(0, 0)] * a.ndim
    pad[-2] = (0, rows - a.shape[-2])
    return jnp.pad(a, pad)


def _trunk_layer(h, past_k, past_v, past_layer, n_past, prefix_kv, s0, pool_buf, pos0, lw, layer,
                 kv_all):
    bsz, t, _ = h.shape
    m = bsz * t
    h1, h1b = _ffn_ln(h.reshape(m, D_MODEL), lw["up1"], lw["down1"], lw["ln_g"][0], lw["ln_b"][0],
                      emit_bf16=True)
    n_prefix = 0 if prefix_kv is None else prefix_kv[0].shape[1]
    n_hist = pool_buf.shape[1]
    hist = jnp.pad(pool_buf, ((0, 0), (POOL_HALO - n_hist, 0), (0, 0)))
    proj, (o_p, pool_tail), ret_in, sb_in, (k_all, v_all) = _in_proj(
        h1b, lw["w_in"], bsz, t, n_prefix, layer, kv_all, hist, n_hist, lw["pool_w"], lw["pool_scale"])
    if prefix_kv is not None:
        front = lambda a: jnp.broadcast_to(a, (bsz,) + a.shape[1:])[None]
        k_all = lax.dynamic_update_slice(k_all, front(prefix_kv[0]), (layer, 0, 0, 0, 0))
        v_all = lax.dynamic_update_slice(v_all, front(prefix_kv[1]), (layer, 0, 0, 0, 0))

    o_r, s_new = _retention(*ret_in, pos0, s0, lw["ret_g"])

    if t % SB_BLOCK:
        sb_in = [_pad_rows(a, SB_BLOCK) for a in sb_in]
    o_s = _stick_breaking(*sb_in, past_k, past_v, past_layer, n_past)[:, :t]

    h2 = _merge(o_r.reshape(m, D_MODEL), o_s.reshape(m, D_MODEL), o_p, proj, h1,
                lw["w_branch"], lw["w_out"], lw["ln_g"][1], lw["ln_b"][1])
    h3 = _ffn_ln(h2, lw["up2"], lw["down2"], lw["ln_g"][2], lw["ln_b"][2], emit_bf16=False)

    assert t >= POOL_HALO
    buf_new = pool_tail[:, POOL_HALO - POOL_BUF:]
    return h3.reshape(bsz, t, D_MODEL), ((k_all, v_all), s_new, buf_new)


def _layer_weights(l, w_in, ret_norm_g, pool_mix_w, pool_scale, w_branch, w_out,
                   ffn1_up, ffn1_down, ffn2_up, ffn2_down, ln_g, ln_b):
    return dict(
        w_in=w_in[l].astype(BF16), ret_g=ret_norm_g[l], pool_w=pool_mix_w[l].astype(BF16),
        pool_scale=pool_scale[l], w_branch=w_branch[l].astype(BF16), w_out=w_out[l].astype(BF16),
        up1=ffn1_up[l].astype(BF16), down1=ffn1_down[l].astype(BF16),
        up2=ffn2_up[l].astype(BF16), down2=ffn2_down[l].astype(BF16),
        ln_g=ln_g[l], ln_b=ln_b[l])


def kernel(x_prompt, x_sample, cache_sb_k, cache_sb_v, state_ret, state_pool, meta_tokens, w_in, ret_norm_g, pool_mix_w, pool_scale, w_branch, w_out, ffn1_up, ffn1_down, ffn2_up, ffn2_down, ln_g, ln_b):
    bsz = x_prompt.shape[0]
    dec_b, past_len = cache_sb_k.shape[1], cache_sb_k.shape[2]
    assert past_len % SB_BLOCK == 0
    h_meta = meta_tokens[None]
    h_p = x_prompt
    h_s = x_sample
    cache_k = cache_sb_k.astype(BF16).transpose(0, 1, 3, 2, 4)
    cache_v = cache_sb_v.astype(BF16).transpose(0, 1, 3, 2, 4)
    no_past = jnp.zeros((1, 1, H_SB, SB_BLOCK, D_SB), BF16)
    outs = [[] for _ in range(4)]
    meta_kv = prompt_kv = sample_kv = None
    for l in range(DEPTH):
        lw = _layer_weights(l, w_in, ret_norm_g, pool_mix_w, pool_scale, w_branch, w_out,
                            ffn1_up, ffn1_down, ffn2_up, ffn2_down, ln_g, ln_b)
        h_meta, (meta_kv, ms, mbuf) = _trunk_layer(
            h_meta, no_past, no_past, 0, 0, None, jnp.zeros((1, H_RET, DK_RET, DV_RET), F32),
            jnp.zeros((1, 0, D_MODEL), F32), 0, lw, l, meta_kv)
        mk, mv = meta_kv[0][l], meta_kv[1][l]
        as_past = lambda a: _pad_rows(a.transpose(0, 2, 1, 3), SB_BLOCK)[None]
        h_p, (prompt_kv, ps, pbuf) = _trunk_layer(
            h_p, as_past(mk), as_past(mv), 0, N_META, (mk, mv), ms, mbuf, N_META, lw, l, prompt_kv)
        h_s, (sample_kv, ss, sbuf) = _trunk_layer(
            h_s, cache_k, cache_v, l, past_len, None, state_ret[l], state_pool[l],
            N_META + past_len, lw, l, sample_kv)
        outs[0].append(ps)
        outs[1].append(pbuf)
        outs[2].append(ss)
        outs[3].append(sbuf)
    new_ret_p, new_pool_p, new_ret_s, new_pool_s = (jnp.stack(o) for o in outs)
    return (h_p, h_s, prompt_kv[0], prompt_kv[1], new_ret_p, new_pool_p,
            sample_kv[0], sample_kv[1], new_ret_s, new_pool_s)
```

```python
import functools
import math

import jax
import jax.numpy as jnp
from jax import lax
from jax.experimental import pallas as pl
from jax.experimental.pallas import tpu as pltpu

D_MODEL = 1024
DEPTH = 2
N_META = 16
H_RET = 4
DK_RET = D_MODEL // 8
DV_RET = 2 * DK_RET
H_SB = 8
D_SB = D_MODEL // 8
RET_CHUNK = 256
POOL_WINDOWS = (2, 4, 8, 16)
DG_POOL = D_MODEL // len(POOL_WINDOWS)
POOL_BUF = max(POOL_WINDOWS) - 1
POOL_HALO = 16
D_FF = ((8 * D_MODEL // 3 + 127) // 128) * 128
D_IN = 10 * D_MODEL
ROPE_BASE = 10000.0
LN_EPS = 1e-5
LOG2E = 1.0 / math.log(2.0)
ALPHA = (2 * DEPTH) ** 0.25

W_COL_QR, W_COL_KR, W_COL_VR, W_COL_GR = 0, 512, 1024, 2048
W_COL_QS, W_COL_KS, W_COL_VS, W_COL_U, W_COL_GATES = 3072, 4096, 5120, 6144, 7168
D_PROJ = D_IN - W_COL_GATES

SB_BLOCK = 128
SB_UNROLL = 2
BF16_ROWS = 16
VMEM_LIMIT = 48 * 1024 * 1024
FFN_VMEM_LIMIT = 56 * 1024 * 1024

F32 = jnp.float32
BF16 = jnp.bfloat16


def _params(*sem):
    return pltpu.CompilerParams(dimension_semantics=sem, vmem_limit_bytes=VMEM_LIMIT)


def _layer_norm(y, g, b):
    mu = jnp.mean(y, axis=-1, keepdims=True)
    d = y - mu
    var = jnp.mean(d * d, axis=-1, keepdims=True)
    return d * lax.rsqrt(var + LN_EPS) * g + b


def _silu(x):
    return x * jax.nn.sigmoid(x)


def _ffn_ln_body(x_ref, wg_ref, wu_ref, wd_ref, g_ref, b_ref, o_ref, *maybe_ob_ref):
    x = x_ref[...]
    xb = x.astype(BF16)
    gate = jnp.dot(xb, wg_ref[...], preferred_element_type=F32)
    up = jnp.dot(xb, wu_ref[...], preferred_element_type=F32)
    act = (_silu(gate) * up).astype(BF16)
    ffn = jnp.dot(act, wd_ref[...], preferred_element_type=F32)
    h = _layer_norm(ALPHA * x + 0.5 * ffn, g_ref[...], b_ref[...])
    o_ref[...] = h
    for ob_ref in maybe_ob_ref:
        ob_ref[...] = h.astype(BF16)


def _ffn_ln(x, w_up, w_down, g, b, emit_bf16):
    m = x.shape[0]
    tm = min(m, 512)
    assert m % tm == 0
    row = lambda i: (i, 0)
    once = pl.Buffered(1)
    out_shape = [jax.ShapeDtypeStruct((m, D_MODEL), F32)]
    out_specs = [pl.BlockSpec((tm, D_MODEL), row)]
    if emit_bf16:
        out_shape.append(jax.ShapeDtypeStruct((m, D_MODEL), BF16))
        out_specs.append(pl.BlockSpec((tm, D_MODEL), row))
    res = pl.pallas_call(
        _ffn_ln_body,
        grid=(m // tm,),
        in_specs=[
            pl.BlockSpec((tm, D_MODEL), row),
            pl.BlockSpec((D_MODEL, D_FF), lambda i: (0, 0), pipeline_mode=once),
            pl.BlockSpec((D_MODEL, D_FF), lambda i: (0, 1), pipeline_mode=once),
            pl.BlockSpec((D_FF, D_MODEL), lambda i: (0, 0), pipeline_mode=once),
            pl.BlockSpec((1, D_MODEL), lambda i: (0, 0)),
            pl.BlockSpec((1, D_MODEL), lambda i: (0, 0)),
        ],
        out_specs=out_specs,
        out_shape=out_shape,
        compiler_params=pltpu.CompilerParams(dimension_semantics=("parallel",),
                                             vmem_limit_bytes=FFN_VMEM_LIMIT),
        name="ffn_ln",
    )(x, w_up, w_up, w_down, g.reshape(1, D_MODEL), b.reshape(1, D_MODEL))
    return res if emit_bf16 else res[0]


def _in_proj_body(x_ref, w_ref, hist_ref, pw_ref, psc_ref, *rest, tm, n_hist):
    (proj_ref, op_ref, tail_ref, qk_ref, vr_ref, gr_ref, qs_ref, kb_ref, vb_ref, kf_ref,
     vf_ref) = rest[-12:-1]
    z_sc = rest[-1]
    i = pl.program_id(1)
    x = x_ref[...]

    def cols(c0, n):
        return jnp.dot(x, w_ref[:, c0:c0 + n], preferred_element_type=F32)

    @pl.when(i == 0)
    def _():
        z_sc[0:POOL_HALO, :] = hist_ref[0]

    z_sc[POOL_HALO:POOL_HALO + tm, :] = cols(W_COL_U, D_MODEL)
    for c0 in range(0, D_PROJ, D_MODEL):
        proj_ref[:, c0:c0 + D_MODEL] = cols(W_COL_GATES + c0, D_MODEL)
    _pool_rows(z_sc, i * tm, tm, n_hist, pw_ref, psc_ref, op_ref)
    tail = z_sc[tm:tm + POOL_HALO, :]
    z_sc[0:POOL_HALO, :] = tail
    tail_ref[0] = tail

    qr, kr = cols(W_COL_QR, H_RET * DK_RET), cols(W_COL_KR, H_RET * DK_RET)
    vr, gr = cols(W_COL_VR, H_RET * DV_RET), cols(W_COL_GR, H_RET * DV_RET)
    for h in range(H_RET):
        qk_ref[0, h, :, 0:DK_RET] = qr[:, h * DK_RET:(h + 1) * DK_RET]
        qk_ref[0, h, :, DK_RET:2 * DK_RET] = kr[:, h * DK_RET:(h + 1) * DK_RET]
        vr_ref[0, h] = vr[:, h * DV_RET:(h + 1) * DV_RET].astype(BF16)
        gr_ref[0, h] = gr[:, h * DV_RET:(h + 1) * DV_RET]
    qs, k, v = cols(W_COL_QS, D_MODEL), cols(W_COL_KS, D_MODEL), cols(W_COL_VS, D_MODEL)
    for h in range(H_SB):
        head = slice(h * D_SB, (h + 1) * D_SB)
        qs_ref[0, h] = qs[:, head].astype(BF16)
        kb_ref[0, h] = k[:, head].astype(BF16)
        vb_ref[0, h] = v[:, head].astype(BF16)
    kf_ref[0, 0] = k.reshape(k.shape[0], H_SB, D_SB)
    vf_ref[0, 0] = v.reshape(v.shape[0], H_SB, D_SB)


def _in_proj(xb, w_in, bsz, t, row0, layer, kv_all, hist, n_hist, pool_w, pool_scale):
    tm = min(t, 256)
    assert t % tm == 0
    nt = t // tm
    per_head = lambda n_heads, width: pl.BlockSpec((1, n_heads, tm, width), lambda b, i: (b, 0, i, 0))
    if row0:
        full = pl.BlockSpec((pl.Element(1), pl.Element(1), pl.Element(tm), pl.Element(H_SB),
                             pl.Element(D_SB)), lambda b, i: (layer, b, row0 + i * tm, 0, 0))
    else:
        full = pl.BlockSpec((1, 1, tm, H_SB, D_SB), lambda b, i: (layer, b, i, 0, 0))
    hist_map = (lambda b, i: (b, 0, 0)) if hist.shape[0] == bsz else (lambda b, i: (0, 0, 0))
    in_specs = [
        pl.BlockSpec((tm, D_MODEL), lambda b, i: (b * nt + i, 0)),
        pl.BlockSpec((D_MODEL, D_IN), lambda b, i: (0, 0), pipeline_mode=pl.Buffered(1)),
        pl.BlockSpec((1, POOL_HALO, D_MODEL), hist_map),
        pl.BlockSpec((len(POOL_WINDOWS), DG_POOL, DG_POOL), lambda b, i: (0, 0, 0)),
        pl.BlockSpec((1, D_MODEL), lambda b, i: (0, 0)),
    ]
    operands = [xb, w_in, hist, pool_w, pool_scale.reshape(1, D_MODEL)]
    aliases = {}
    if kv_all is not None:
        in_specs += [pl.BlockSpec(memory_space=pl.ANY)] * 2
        operands += list(kv_all)
        aliases = {5: 9, 6: 10}
    rows = lambda width: pl.BlockSpec((tm, width), lambda b, i: (b * nt + i, 0))
    res = pl.pallas_call(
        functools.partial(_in_proj_body, tm=tm, n_hist=n_hist),
        grid=(bsz, nt),
        in_specs=in_specs,
        out_specs=[rows(D_PROJ), rows(D_MODEL),
                   pl.BlockSpec((1, POOL_HALO, D_MODEL), lambda b, i: (b, 0, 0)),
                   per_head(H_RET, 2 * DK_RET), per_head(H_RET, DV_RET), per_head(H_RET, DV_RET),
                   per_head(H_SB, D_SB), per_head(H_SB, D_SB), per_head(H_SB, D_SB), full, full],
        out_shape=[jax.ShapeDtypeStruct((bsz * t, D_PROJ), F32),
                   jax.ShapeDtypeStruct((bsz * t, D_MODEL), BF16),
                   jax.ShapeDtypeStruct((bsz, POOL_HALO, D_MODEL), F32),
                   jax.ShapeDtypeStruct((bsz, H_RET, t, 2 * DK_RET), F32),
                   jax.ShapeDtypeStruct((bsz, H_RET, t, DV_RET), BF16),
                   jax.ShapeDtypeStruct((bsz, H_RET, t, DV_RET), F32)]
        + [jax.ShapeDtypeStruct((bsz, H_SB, t, D_SB), BF16)] * 3
        + [jax.ShapeDtypeStruct((DEPTH, bsz, row0 + t, H_SB, D_SB), F32)] * 2,
        input_output_aliases=aliases,
        scratch_shapes=[pltpu.VMEM((POOL_HALO + tm, D_MODEL), F32)],
        compiler_params=_params("parallel", "arbitrary"),
        name="in_proj",
    )(*operands)
    proj, o_p, tail, qk_r, v_r, g_r, q_s, k_s, v_s, k_all, v_all = res
    return proj, (o_p, tail), (qk_r, v_r, g_r), (q_s, k_s, v_s), (k_all, v_all)


def _retention_body(qk_ref, v_ref, gr_ref, cos_ref, sin_ref, s0_ref, dintra_ref, dq_ref,
                    dk_ref, dc_ref, rg_ref, o_ref, snew_ref, qr_sc, kr_sc, kd_sc, s_sc,
                    *, chunk, n_chunks):
    d_intra = dintra_ref[0]
    d_q = dq_ref[0]
    d_k = dk_ref[0]
    d_c = dc_ref[0]
    rg = rg_ref[0]

    def rot(x, cos, sin):
        return x * cos + pltpu.roll(x, DK_RET // 2, 1) * sin

    for c in range(n_chunks):
        rows = slice(c * chunk, (c + 1) * chunk)
        cos = cos_ref[rows, :]
        sin = sin_ref[rows, :]
        k = rot(qk_ref[rows, DK_RET:2 * DK_RET], cos, sin) * (DK_RET ** -0.5)
        qr_sc[rows, :] = rot(qk_ref[rows, 0:DK_RET], cos, sin).astype(BF16)
        kr_sc[rows, :] = k.astype(BF16)
        kd_sc[rows, :] = (k * d_k).astype(BF16)

    s = s0_ref[0, 0]
    for c in range(n_chunks):
        rows = slice(c * chunk, (c + 1) * chunk)
        s_sc[c] = s.astype(BF16)
        s = s * d_c + lax.dot_general(kd_sc[rows, :], v_ref[rows, :], (((0,), (0,)), ((), ())),
                                      preferred_element_type=F32)
    snew_ref[0, 0] = s

    def products(c):
        rows = slice(c * chunk, (c + 1) * chunk)
        qc = qr_sc[rows, :]
        att = lax.dot_general(qc, kr_sc[rows, :], (((1,), (1,)), ((), ())),
                              preferred_element_type=F32)
        return att, jnp.dot(qc, s_sc[c], preferred_element_type=F32)

    nxt = products(0)
    for c in range(n_chunks):
        rows = slice(c * chunk, (c + 1) * chunk)
        att, inter = nxt
        if c + 1 < n_chunks:
            nxt = products(c + 1)
        o = (jnp.dot((att * d_intra).astype(BF16), v_ref[rows, :], preferred_element_type=F32)
             + inter * d_q)
        mu = jnp.mean(o, axis=-1, keepdims=True)
        d = o - mu
        var = jnp.mean(d * d, axis=-1, keepdims=True)
        on = d * lax.rsqrt(var + LN_EPS) * rg
        o_ref[0, rows, :] = (on * _silu(gr_ref[rows, :])).astype(o_ref.dtype)


def _retention_decays(chunk):
    lg = jnp.log1p(-jnp.exp(jnp.linspace(math.log(1.0 / 32), math.log(1.0 / 512), H_RET, dtype=F32)))
    i = jnp.arange(chunk, dtype=F32)
    diff = i[:, None] - i[None, :]
    d_intra = jnp.where(diff >= 0, jnp.exp(lg[:, None, None] * jnp.maximum(diff, 0.0)), 0.0)
    d_q = jnp.exp(lg[:, None] * (i[None, :] + 1.0))
    d_k = jnp.exp(lg[:, None] * (chunk - 1.0 - i[None, :]))
    d_c = jnp.exp(lg * chunk)
    return (d_intra,
            jnp.broadcast_to(d_q[:, :, None], (H_RET, chunk, DV_RET)),
            jnp.broadcast_to(d_k[:, :, None], (H_RET, chunk, DK_RET)),
            jnp.broadcast_to(d_c[:, None, None], (H_RET, 1, DV_RET)))


def _rotary_tables(pos0, t):
    half = DK_RET // 2
    inv_freq = ROPE_BASE ** (-jnp.arange(half, dtype=F32) / half)
    ang = (pos0 + jnp.arange(t, dtype=F32))[:, None] * inv_freq[None, :]
    cos = jnp.cos(ang)
    sin = jnp.sin(ang)
    return jnp.concatenate([cos, cos], axis=-1), jnp.concatenate([-sin, sin], axis=-1)


def _retention(qk_r, v_r, g_r, pos0, s0, ret_g):
    bsz, _, t, _ = qk_r.shape
    chunk = min(RET_CHUNK, t)
    n_chunks = t // chunk
    cos, sin = _rotary_tables(pos0, t)
    d_intra, d_q, d_k, d_c = _retention_decays(chunk)
    s0_map = (lambda b, h: (b, h, 0, 0)) if s0.shape[0] == bsz else (lambda b, h: (0, h, 0, 0))
    per_head = lambda b, h: (h, 0, 0)
    return pl.pallas_call(
        functools.partial(_retention_body, chunk=chunk, n_chunks=n_chunks),
        grid=(bsz, H_RET),
        in_specs=[
            pl.BlockSpec((None, None, t, 2 * DK_RET), lambda b, h: (b, h, 0, 0)),
            pl.BlockSpec((None, None, t, DV_RET), lambda b, h: (b, h, 0, 0)),
            pl.BlockSpec((None, None, t, DV_RET), lambda b, h: (b, h, 0, 0)),
            pl.BlockSpec((t, DK_RET), lambda b, h: (0, 0)),
            pl.BlockSpec((t, DK_RET), lambda b, h: (0, 0)),
            pl.BlockSpec((1, 1, DK_RET, DV_RET), s0_map),
            pl.BlockSpec((1, chunk, chunk), per_head),
            pl.BlockSpec((1, chunk, DV_RET), per_head),
            pl.BlockSpec((1, chunk, DK_RET), per_head),
            pl.BlockSpec((1, 1, DV_RET), per_head),
            pl.BlockSpec((1, 1, DV_RET), per_head),
        ],
        out_specs=[
            pl.BlockSpec((1, t, DV_RET), lambda b, h: (b, 0, h)),
            pl.BlockSpec((1, 1, DK_RET, DV_RET), lambda b, h: (b, h, 0, 0)),
        ],
        out_shape=[
            jax.ShapeDtypeStruct((bsz, t, H_RET * DV_RET), BF16),
            jax.ShapeDtypeStruct((bsz, H_RET, DK_RET, DV_RET), F32),
        ],
        scratch_shapes=[pltpu.VMEM((t, DK_RET), BF16), pltpu.VMEM((t, DK_RET), BF16),
                        pltpu.VMEM((t, DK_RET), BF16),
                        pltpu.VMEM((n_chunks, DK_RET, DV_RET), BF16)],
        compiler_params=_params("parallel", "parallel"),
        name="retention",
    )(qk_r, v_r, g_r, cos, sin, s0, d_intra, d_q, d_k, d_c, ret_g.reshape(H_RET, 1, DV_RET))


def _stick_breaking_body(itab_ref, gtab_ref, q_ref, kn_ref, vn_ref, kp_ref, vp_ref, o_ref,
                         qt_sc, kn_sc, vnt_sc, kp_sc, vpt_sc, kr_sc, vrt_sc, acc_sc, carry_sc,
                         z_sc, hilo_sc, incl_sc, *, qb, n_q, n_past_valid):
    kb = SB_BLOCK
    pg = 2 * kb
    n_pfull = n_past_valid // pg
    n_prem = n_past_valid - n_pfull * pg
    n_prem_blocks = -(-n_prem // kb)
    scale = D_SB ** -0.5

    transposed = lambda a: a.astype(F32).T.astype(BF16)
    kn_sc[...] = kn_ref[...].astype(BF16)
    for g in range(n_q):
        rows = slice(g * qb, (g + 1) * qb)
        vnt_sc[g] = transposed(vn_ref[rows, :])
        qt_sc[g] = transposed(q_ref[rows, :])
    if n_pfull:
        kp_sc[...] = kp_ref[0:n_pfull * pg, :].astype(BF16)
        for g in range(n_pfull):
            vpt_sc[g] = transposed(vp_ref[g * pg:(g + 1) * pg, :])
    for j in range(n_prem_blocks):
        rows = slice(n_pfull * pg + j * kb, n_pfull * pg + (j + 1) * kb)
        kr_sc[j] = kp_ref[rows, :].astype(BF16)
        vrt_sc[j] = transposed(vp_ref[rows, :])
    acc_sc[...] = jnp.zeros_like(acc_sc)
    carry_sc[...] = jnp.zeros_like(carry_sc)

    def suffix_ones2(sb):
        ones = (lax.broadcasted_iota(jnp.int32, (sb, sb), 1)
                >= lax.broadcasted_iota(jnp.int32, (sb, sb), 0)).astype(BF16)
        return jnp.concatenate([ones, ones], axis=1)

    last_valid = n_prem - (n_prem_blocks - 1) * kb if n_prem_blocks else kb
    last_rows = -(-last_valid // BF16_ROWS) * BF16_ROWS
    suffix_mats = {sb: suffix_ones2(sb) for sb in {kb, last_rows}}

    def stage_a(k_grp, q_t):
        return jnp.dot(k_grp, q_t, preferred_element_type=F32) * (scale * LOG2E)

    def stage_b(z, mask):
        neg_abs = lax.bitcast_convert_type(
            lax.bitcast_convert_type(z, jnp.uint32) | jnp.uint32(0x80000000), F32)
        go = jnp.maximum(z, 0.0) + jnp.log(1.0 + jnp.exp2(neg_abs)) * LOG2E
        if mask is not None:
            go = jnp.where(mask, go, 0.0)
        hi = go.astype(BF16)
        lo = (go - hi.astype(F32)).astype(BF16)
        sb = min(kb, z.shape[0])
        parts = []
        for s in range(z.shape[0] // sb):
            parts += [hi[s * sb:(s + 1) * sb], lo[s * sb:(s + 1) * sb]]
        return jnp.concatenate(parts, axis=0)

    def stage_c(hilo):
        sb = min(kb, hilo.shape[0] // 2)
        parts = [jnp.dot(suffix_mats[sb], hilo[s * 2 * sb:(s + 1) * 2 * sb],
                         preferred_element_type=F32) for s in range(hilo.shape[0] // (2 * sb))]
        return parts[0] if len(parts) == 1 else jnp.concatenate(parts, axis=0)

    def stage_de(z, incl, mask, vt_grp, qi):
        carry = carry_sc[qi]
        sb = min(kb, z.shape[0])
        n_sub = z.shape[0] // sb
        ws = [None] * n_sub
        for s in reversed(range(n_sub)):
            rows = slice(s * sb, (s + 1) * sb)
            w = jnp.exp2(z[rows] - (incl[rows] + carry))
            if mask is not None:
                w = jnp.where(mask[rows], w, 0.0)
            ws[s] = w.astype(BF16)
            carry = carry + incl[s * sb:s * sb + 1, :]
        carry_sc[qi] = carry
        w_all = ws[0] if n_sub == 1 else jnp.concatenate(ws, axis=0)
        acc_sc[qi] += jnp.dot(vt_grp, w_all, preferred_element_type=F32)

    def run_unrolled(tiles):
        n = len(tiles)
        z, hilo, incl = {}, {}, {}
        for tau in range(n + 2):
            if tau < n:
                k_fn, _, qi, mask = tiles[tau]
                z[tau] = stage_a(k_fn(), qt_sc[qi])
            if 1 <= tau <= n:
                incl[tau - 1] = stage_c(hilo.pop(tau - 1))
            if tau >= 2:
                _, vt_fn, qi2, mask2 = tiles[tau - 2]
                stage_de(z.pop(tau - 2), incl.pop(tau - 2), mask2, vt_fn(), qi2)
            if tau < n:
                hilo[tau] = stage_b(z[tau], mask)

    def run_loop(n, base, k_of, vt_of):
        u = SB_UNROLL if n % SB_UNROLL == 0 and n >= 2 * SB_UNROLL else 1
        n_steps = n // u
        tile = lambda step, k: base + step * u + k

        def a_all(step):
            return [stage_a(k_of(gtab_ref[tile(step, k)]), qt_sc[itab_ref[tile(step, k)]])
                    for k in range(u)]

        def c_all(slot):
            return [stage_c(hilo_sc[slot * u + k]) for k in range(u)]

        def b_store(zs, slot):
            for k in range(u):
                z_sc[slot * u + k] = zs[k]
                hilo_sc[slot * u + k] = stage_b(zs[k], None)

        def c_store(incs, slot):
            for k in range(u):
                incl_sc[slot * u + k] = incs[k]

        def finish(step, slot):
            for k in range(u):
                stage_de(z_sc[slot * u + k], incl_sc[slot * u + k], None,
                         vt_of(gtab_ref[tile(step, k)]), itab_ref[tile(step, k)])

        b_store(a_all(0), 0)
        zs = a_all(1)
        incs = c_all(0)
        b_store(zs, 1)
        c_store(incs, 0)

        def body(step, c):
            slot = step % 2
            zs = a_all(step)
            incs = c_all(1 - slot)
            finish(step - 2, slot)
            b_store(zs, slot)
            c_store(incs, 1 - slot)
            return c

        lax.fori_loop(2, n_steps, body, 0)
        slot = n_steps % 2
        incs = c_all(1 - slot)
        finish(n_steps - 2, slot)
        c_store(incs, 1 - slot)
        finish(n_steps - 1, 1 - slot)

    diag_mask = (lax.broadcasted_iota(jnp.int32, (qb, qb), 0)
                 < lax.broadcasted_iota(jnp.int32, (qb, qb), 1))
    run_unrolled([((lambda g=g: kn_sc[g * qb:(g + 1) * qb, :]), (lambda g=g: vnt_sc[g]), g, diag_mask)
                  for g in range(n_q)])
    n_earlier = n_q * (n_q - 1) // 2
    new_k = lambda g: kn_sc[pl.ds(pl.multiple_of(g * qb, qb), qb), :]
    if n_earlier == 1:
        run_unrolled([((lambda: kn_sc[0:qb, :]), (lambda: vnt_sc[0]), 1, None)])
    elif n_earlier:
        run_loop(n_earlier, 0, new_k, lambda g: vnt_sc[g])

    rem_tiles = []
    for j in reversed(range(n_prem_blocks)):
        valid, rows = (last_valid, last_rows) if j == n_prem_blocks - 1 else (kb, kb)
        mask = None if valid == rows else lax.broadcasted_iota(jnp.int32, (rows, qb), 0) < valid
        if rows <= 2 * BF16_ROWS:
            if rem_tiles:
                run_unrolled(rem_tiles)
                rem_tiles = []
            k_blk, vt_blk = kr_sc[j, 0:rows, :], vrt_sc[j, :, 0:rows]
            z = jnp.concatenate([stage_a(k_blk, qt_sc[qi]) for qi in range(n_q)], axis=1)
            wide_mask = None if mask is None else jnp.concatenate([mask] * n_q, axis=1)
            incl = stage_c(stage_b(z, wide_mask))
            for qi in range(n_q):
                lanes = slice(qi * qb, (qi + 1) * qb)
                stage_de(z[:, lanes], incl[:, lanes], mask, vt_blk, qi)
            continue
        rem_tiles += [((lambda j=j, rows=rows: kr_sc[j, 0:rows, :]),
                       (lambda j=j, rows=rows: vrt_sc[j, :, 0:rows]), qi, mask) for qi in range(n_q)]
    if rem_tiles:
        run_unrolled(rem_tiles)
    n_ptiles = n_q * n_pfull
    past_k = lambda g: kp_sc[pl.ds(pl.multiple_of(g * pg, pg), pg), :]
    if n_ptiles == 1:
        run_unrolled([((lambda: kp_sc[0:pg, :]), (lambda: vpt_sc[0]), 0, None)])
    elif n_ptiles:
        run_loop(n_ptiles, n_earlier, past_k, lambda g: vpt_sc[g])

    for g in range(n_q):
        o_ref[0, g * qb:(g + 1) * qb, :] = acc_sc[g].T.astype(o_ref.dtype)


def _stick_breaking(q_arr, k_arr, v_arr, past_k, past_v, past_layer, n_past_valid):
    bsz, _, tq, _ = q_arr.shape
    p = past_k.shape[3]
    qb = 2 * SB_BLOCK if tq % (2 * SB_BLOCK) == 0 else SB_BLOCK
    n_q = tq // qb
    pg = 2 * SB_BLOCK
    n_pfull = n_past_valid // pg
    n_prem_blocks = -(-(n_past_valid - n_pfull * pg) // SB_BLOCK)
    assert k_arr.shape[2] == tq and p % SB_BLOCK == 0 and n_past_valid <= p
    tiles = [(i, g) for i in range(n_q) for g in reversed(range(i))]
    tiles += [(i, g) for i in range(n_q) for g in reversed(range(n_pfull))]
    tiles = tiles or [(0, 0)]
    itab = jnp.asarray([t[0] for t in tiles], jnp.int32)
    gtab = jnp.asarray([t[1] for t in tiles], jnp.int32)
    past_map = ((lambda b, h, it, gt: (past_layer, b, h, 0, 0)) if past_k.shape[1] == bsz
                else (lambda b, h, it, gt: (past_layer, 0, h, 0, 0)))
    per_head = pl.BlockSpec((None, None, tq, D_SB), lambda b, h, it, gt: (b, h, 0, 0))
    grid_spec = pltpu.PrefetchScalarGridSpec(
        num_scalar_prefetch=2,
        grid=(bsz, H_SB),
        in_specs=[
            per_head, per_head, per_head,
            pl.BlockSpec((None, None, None, p, D_SB), past_map),
            pl.BlockSpec((None, None, None, p, D_SB), past_map),
        ],
        out_specs=pl.BlockSpec((1, tq, D_SB), lambda b, h, it, gt: (b, 0, h)),
        scratch_shapes=[
            pltpu.VMEM((n_q, D_SB, qb), BF16),
            pltpu.VMEM((tq, D_SB), BF16),
            pltpu.VMEM((n_q, D_SB, qb), BF16),
            pltpu.VMEM((max(n_pfull, 1) * pg, D_SB), BF16),
            pltpu.VMEM((max(n_pfull, 1), D_SB, pg), BF16),
            pltpu.VMEM((max(n_prem_blocks, 1), SB_BLOCK, D_SB), BF16),
            pltpu.VMEM((max(n_prem_blocks, 1), D_SB, SB_BLOCK), BF16),
            pltpu.VMEM((n_q, D_SB, qb), F32),
            pltpu.VMEM((n_q, 1, qb), F32),
            pltpu.VMEM((2 * SB_UNROLL, max(qb, pg), qb), F32),
            pltpu.VMEM((2 * SB_UNROLL, 2 * max(qb, pg), qb), BF16),
            pltpu.VMEM((2 * SB_UNROLL, max(qb, pg), qb), F32),
        ])
    return pl.pallas_call(
        functools.partial(_stick_breaking_body, qb=qb, n_q=n_q, n_past_valid=n_past_valid),
        grid_spec=grid_spec,
        out_shape=jax.ShapeDtypeStruct((bsz, tq, H_SB * D_SB), BF16),
        compiler_params=_params("parallel", "parallel"),
        name="stick_breaking",
    )(itab, gtab, q_arr, k_arr, v_arr, past_k, past_v)


def _pool_rows(z_sc, first_row, tm, n_hist, w_ref, sc_ref, o_ref):
    tr = min(tm, 256)
    for g, win in enumerate(POOL_WINDOWS):
        cols = slice(g * DG_POOL, (g + 1) * DG_POOL)
        for r0 in range(0, tm, tr):
            base = POOL_HALO + r0
            wsum = z_sc[base:base + tr, cols]
            for back in range(1, win):
                wsum = wsum + z_sc[base - back:base - back + tr, cols]
            t_idx = first_row + r0 + lax.broadcasted_iota(jnp.int32, (tr, 1), 0)
            cnt = jnp.minimum(t_idx + (n_hist + 1), win).astype(F32)
            pooled = wsum / cnt - z_sc[base:base + tr, cols]
            mixed = jnp.dot(pooled.astype(BF16), w_ref[g], preferred_element_type=F32)
            o_ref[r0:r0 + tr, cols] = (mixed * sc_ref[:, cols]).astype(o_ref.dtype)


def _merge_body(or_ref, os_ref, op_ref, g0_ref, g1_ref, g2_ref, h_ref, wb_ref, wo_ref, g_ref, b_ref,
                o_ref):
    mix = jax.nn.sigmoid(g0_ref[...]) * jnp.dot(or_ref[...], wb_ref[0], preferred_element_type=F32)
    mix = mix + jax.nn.sigmoid(g1_ref[...]) * jnp.dot(os_ref[...], wb_ref[1],
                                                      preferred_element_type=F32)
    mix = mix + jax.nn.sigmoid(g2_ref[...]) * jnp.dot(op_ref[...], wb_ref[2],
                                                      preferred_element_type=F32)
    out = jnp.dot(mix.astype(BF16), wo_ref[...], preferred_element_type=F32)
    o_ref[...] = _layer_norm(ALPHA * h_ref[...] + out, g_ref[...], b_ref[...])


def _merge(o_r, o_s, o_p, proj, h, w_branch, w_out, g, b):
    m = h.shape[0]
    tm = min(m, 512)
    row = lambda i: (i, 0)
    gcol = 0
    return pl.pallas_call(
        _merge_body,
        grid=(m // tm,),
        in_specs=[
            pl.BlockSpec((tm, D_MODEL), row),
            pl.BlockSpec((tm, D_MODEL), row),
            pl.BlockSpec((tm, D_MODEL), row),
            pl.BlockSpec((tm, D_MODEL), lambda i: (i, gcol)),
            pl.BlockSpec((tm, D_MODEL), lambda i: (i, gcol + 1)),
            pl.BlockSpec((tm, D_MODEL), lambda i: (i, gcol + 2)),
            pl.BlockSpec((tm, D_MODEL), row),
            pl.BlockSpec((3, D_MODEL, D_MODEL), lambda i: (0, 0, 0)),
            pl.BlockSpec((D_MODEL, D_MODEL), lambda i: (0, 0)),
            pl.BlockSpec((1, D_MODEL), lambda i: (0, 0)),
            pl.BlockSpec((1, D_MODEL), lambda i: (0, 0)),
        ],
        out_specs=pl.BlockSpec((tm, D_MODEL), row),
        out_shape=jax.ShapeDtypeStruct((m, D_MODEL), F32),
        compiler_params=_params("parallel"),
        name="merge",
    )(o_r, o_s, o_p, proj, proj, proj, h, w_branch, w_out, g.reshape(1, D_MODEL),
      b.reshape(1, D_MODEL))


def _pad_rows(a, rows):
    pad = [(0, 0)] * a.ndim
    pad[-2] = (0, rows - a.shape[-2])
    return jnp.pad(a, pad)


def _trunk_layer(h, past_k, past_v, past_layer, n_past, prefix_kv, s0, pool_buf, pos0, lw, layer,
                 kv_all):
    bsz, t, _ = h.shape
    m = bsz * t
    h1, h1b = _ffn_ln(h.reshape(m, D_MODEL), lw["up1"], lw["down1"], lw["ln_g"][0], lw["ln_b"][0],
                      emit_bf16=True)
    n_prefix = 0 if prefix_kv is None else prefix_kv[0].shape[1]
    n_hist = pool_buf.shape[1]
    hist = jnp.pad(pool_buf, ((0, 0), (POOL_HALO - n_hist, 0), (0, 0)))
    proj, (o_p, pool_tail), ret_in, sb_in, (k_all, v_all) = _in_proj(
        h1b, lw["w_in"], bsz, t, n_prefix, layer, kv_all, hist, n_hist, lw["pool_w"], lw["pool_scale"])
    if prefix_kv is not None:
        front = lambda a: jnp.broadcast_to(a, (bsz,) + a.shape[1:])[None]
        k_all = lax.dynamic_update_slice(k_all, front(prefix_kv[0]), (layer, 0, 0, 0, 0))
        v_all = lax.dynamic_update_slice(v_all, front(prefix_kv[1]), (layer, 0, 0, 0, 0))

    o_r, s_new = _retention(*ret_in, pos0, s0, lw["ret_g"])

    if t % SB_BLOCK:
        sb_in = [_pad_rows(a, SB_BLOCK) for a in sb_in]
    o_s = _stick_breaking(*sb_in, past_k, past_v, past_layer, n_past)[:, :t]

    h2 = _merge(o_r.reshape(m, D_MODEL), o_s.reshape(m, D_MODEL), o_p, proj, h1,
                lw["w_branch"], lw["w_out"], lw["ln_g"][1], lw["ln_b"][1])
    h3 = _ffn_ln(h2, lw["up2"], lw["down2"], lw["ln_g"][2], lw["ln_b"][2], emit_bf16=False)

    assert t >= POOL_HALO
    buf_new = pool_tail[:, POOL_HALO - POOL_BUF:]
    return h3.reshape(bsz, t, D_MODEL), ((k_all, v_all), s_new, buf_new)


def _layer_weights(l, w_in, ret_norm_g, pool_mix_w, pool_scale, w_branch, w_out,
                   ffn1_up, ffn1_down, ffn2_up, ffn2_down, ln_g, ln_b):
    return dict(
        w_in=w_in[l].astype(BF16), ret_g=ret_norm_g[l], pool_w=pool_mix_w[l].astype(BF16),
        pool_scale=pool_scale[l], w_branch=w_branch[l].astype(BF16), w_out=w_out[l].astype(BF16),
        up1=ffn1_up[l].astype(BF16), down1=ffn1_down[l].astype(BF16),
        up2=ffn2_up[l].astype(BF16), down2=ffn2_down[l].astype(BF16),
        ln_g=ln_g[l], ln_b=ln_b[l])


def kernel(x_prompt, x_sample, cache_sb_k, cache_sb_v, state_ret, state_pool, meta_tokens, w_in, ret_norm_g, pool_mix_w, pool_scale, w_branch, w_out, ffn1_up, ffn1_down, ffn2_up, ffn2_down, ln_g, ln_b):
    bsz = x_prompt.shape[0]
    dec_b, past_len = cache_sb_k.shape[1], cache_sb_k.shape[2]
    assert past_len % SB_BLOCK == 0
    h_meta = meta_tokens[None]
    h_p = x_prompt
    h_s = x_sample
    cache_k = cache_sb_k.astype(BF16).transpose(0, 1, 3, 2, 4)
    cache_v = cache_sb_v.astype(BF16).transpose(0, 1, 3, 2, 4)
    no_past = jnp.zeros((1, 1, H_SB, SB_BLOCK, D_SB), BF16)
    outs = [[] for _ in range(4)]
    meta_kv = prompt_kv = sample_kv = None
    for l in range(DEPTH):
        lw = _layer_weights(l, w_in, ret_norm_g, pool_mix_w, pool_scale, w_branch, w_out,
                            ffn1_up, ffn1_down, ffn2_up, ffn2_down, ln_g, ln_b)
        h_meta, (meta_kv, ms, mbuf) = _trunk_layer(
            h_meta, no_past, no_past, 0, 0, None, jnp.zeros((1, H_RET, DK_RET, DV_RET), F32),
            jnp.zeros((1, 0, D_MODEL), F32), 0, lw, l, meta_kv)
        mk, mv = meta_kv[0][l], meta_kv[1][l]
        as_past = lambda a: _pad_rows(a.transpose(0, 2, 1, 3), SB_BLOCK)[None]
        h_p, (prompt_kv, ps, pbuf) = _trunk_layer(
            h_p, as_past(mk), as_past(mv), 0, N_META, (mk, mv), ms, mbuf, N_META, lw, l, prompt_kv)
        h_s, (sample_kv, ss, sbuf) = _trunk_layer(
            h_s, cache_k, cache_v, l, past_len, None, state_ret[l], state_pool[l],
            N_META + past_len, lw, l, sample_kv)
        outs[0].append(ps)
        outs[1].append(pbuf)
        outs[2].append(ss)
        outs[3].append(sbuf)
    new_ret_p, new_pool_p, new_ret_s, new_pool_s = (jnp.stack(o) for o in outs)
    return (h_p, h_s, prompt_kv[0], prompt_kv[1], new_ret_p, new_pool_p,
            sample_kv[0], sample_kv[1], new_ret_s, new_pool_s)
```
